```python
import functools
import jax, jax.numpy as jnp
from jax import lax
import numpy as np

D_MODEL = 2048
BATCH = 4
SEQ = 2048
DEPTH = 4
DEC_BATCH = 128
DEC_SEQ = 1
PAST_LEN = 8192
PAGE_SIZE = 128

HEAD_DIM = 128
H_RET = 4
H_ML = 4
H_MLA = 8
D_RET = H_RET * HEAD_DIM
D_ML = H_ML * HEAD_DIM
D_MLA = H_MLA * HEAD_DIM
D_MIX = D_RET + D_ML + D_MLA
Q_RANK = 512
KV_RANK = 256
D_NOPE = 128
D_ROPE = 64
D_V = HEAD_DIM
ROPE_BASE = 10000.0
MIX_CHUNK = 128
Q_BLOCK = 128
NORM_EPS = 1e-6
MLSTM_EPS = 1e-6
IN_WIDTHS = (D_RET, D_RET, D_RET, D_RET,
             D_ML, D_ML, D_ML, D_ML, D_ML, H_ML, H_ML,
             Q_RANK, KV_RANK, D_ROPE, D_MLA)
D_IN = sum(IN_WIDTHS)

kernel_name = "hybrid_retention_mlstm_mla_step"

F32 = jnp.float32


def rms_norm(x, g):
    xf = x.astype(F32)
    y = xf * lax.rsqrt(jnp.mean(xf * xf, -1, keepdims=True) + NORM_EPS)
    return (y * g.astype(F32)).astype(x.dtype)


def head_norm(o, g):
    o = o.astype(F32)
    mu = jnp.mean(o, -1, keepdims=True)
    var = jnp.mean(jnp.square(o - mu), -1, keepdims=True)
    y = (o - mu) * lax.rsqrt(var + NORM_EPS)
    b, l, h, d = o.shape
    return y.reshape(b, l, h * d) * g.astype(F32)


def rope(x, pos):
    half = x.shape[-1] // 2
    inv = ROPE_BASE ** (-jnp.arange(half, dtype=F32) / half)
    ang = pos.astype(F32)[:, None] * inv[None, :]
    cos, sin = jnp.cos(ang)[:, None, :], jnp.sin(ang)[:, None, :]
    xf = x.astype(F32)
    x1, x2 = xf[..., :half], xf[..., half:]
    return jnp.concatenate([x1 * cos - x2 * sin, x1 * sin + x2 * cos], -1).astype(x.dtype)


def chunk_len(L):
    return MIX_CHUNK if L % MIX_CHUNK == 0 else L


def to_chunks(t, n, c):
    b, l, h, d = t.shape
    return t.astype(F32).reshape(b, n, c, h, d).transpose(1, 0, 3, 2, 4)


def from_chunks(t):
    n, b, h, c, d = t.shape
    return t.transpose(1, 0, 3, 2, 4).reshape(b, n * c, h, d)


def retention(q, k, v, s0):
    L = q.shape[1]
    c = chunk_len(L)
    n = L // c
    log_g = jnp.log1p(-jnp.exp2(-5.0 - jnp.arange(H_RET, dtype=F32)))
    idx = jnp.arange(c, dtype=F32)
    rel = idx[:, None] - idx[None, :]
    decay = jnp.where(rel >= 0, jnp.exp(log_g[:, None, None] * jnp.maximum(rel, 0.0)), 0.0)
    q_dec = jnp.exp(log_g[:, None] * (idx + 1.0))[..., None]
    k_dec = jnp.exp(log_g[:, None] * (c - 1.0 - idx))[..., None]
    chunk_dec = jnp.exp(log_g * c)[:, None, None]

    def step(S, inp):
        qb, kb, vb = inp
        sc = jnp.einsum('bhik,bhjk->bhij', qb, kb) * decay
        o = jnp.einsum('bhij,bhjv->bhiv', sc, vb) + jnp.einsum('bhik,bhkv->bhiv', qb, S) * q_dec
        S = S * chunk_dec + jnp.einsum('bhjk,bhjv->bhkv', kb * k_dec, vb)
        return S, o

    S, o = lax.scan(step, s0.astype(F32), (to_chunks(q, n, c), to_chunks(k, n, c), to_chunks(v, n, c)))
    return from_chunks(o), S


def mlstm(q, k, v, i_pre, f_pre, c0, n0, m0):
    L = q.shape[1]
    c = chunk_len(L)
    n = L // c
    ic = to_chunks(i_pre[..., None], n, c)[..., 0]
    lfc = to_chunks(jax.nn.log_sigmoid(f_pre.astype(F32))[..., None], n, c)[..., 0]
    causal = jnp.tril(jnp.ones((c, c), dtype=bool))

    def step(carry, inp):
        C, nv, m = carry
        qb, kb, vb, ib, lf = inp
        b = jnp.cumsum(lf, -1)
        log_d = jnp.where(causal, b[..., :, None] - b[..., None, :] + ib[..., None, :], -jnp.inf)
        inter = b + m[..., None]
        m_t = jnp.maximum(inter, jnp.max(log_d, -1))
        w_intra = jnp.exp(log_d - m_t[..., None])
        w_inter = jnp.exp(inter - m_t)
        s = jnp.einsum('bhik,bhjk->bhij', qb, kb) * w_intra
        num = jnp.einsum('bhij,bhjv->bhiv', s, vb) + w_inter[..., None] * jnp.einsum('bhik,bhkv->bhiv', qb, C)
        qn = jnp.sum(s, -1) + w_inter * jnp.einsum('bhik,bhk->bhi', qb, nv)
        hb = num / (jnp.maximum(jnp.abs(qn), jnp.exp(-m_t)) + MLSTM_EPS)[..., None]
        m_new = m_t[..., -1]
        w_k = jnp.exp(b[..., -1:] - b + ib - m_new[..., None])
        dec = jnp.exp(b[..., -1] + m - m_new)
        C = dec[..., None, None] * C + jnp.einsum('bhj,bhjk,bhjv->bhkv', w_k, kb, vb)
        nv = dec[..., None] * nv + jnp.einsum('bhj,bhjk->bhk', w_k, kb)
        return (C, nv, m_new), hb

    (C, nv, m), h = lax.scan(step, (c0.astype(F32), n0.astype(F32), m0.astype(F32)),
                             (to_chunks(q, n, c), to_chunks(k, n, c), to_chunks(v, n, c), ic, lfc))
    return from_chunks(h), C, nv, m


def mla_prompt_attend(q_nope, q_rope, c_kv, k_rope, w_uk, w_uv):
    B, L, H, _ = q_nope.shape
    k_nope = jnp.einsum('blr,rhd->bhld', c_kv, w_uk)
    v = jnp.einsum('blr,rhd->bhld', c_kv, w_uv)
    qb = min(Q_BLOCK, L)
    nb = L // qb
    qn = q_nope.reshape(B, nb, qb, H, D_NOPE).transpose(1, 0, 3, 2, 4)
    qr = q_rope.reshape(B, nb, qb, H, D_ROPE).transpose(1, 0, 3, 2, 4)
    key_pos = jnp.arange(L)
    scale = (D_NOPE + D_ROPE) ** -0.5

    def blk(args):
        i, qn_b, qr_b = args
        s = (jnp.einsum('bhqd,bhkd->bhqk', qn_b, k_nope)
             + jnp.einsum('bhqd,bkd->bhqk', qr_b, k_rope)).astype(F32) * scale
        qpos = i * qb + jnp.arange(qb)
        s = jnp.where(key_pos[None, :] <= qpos[:, None], s, -jnp.inf)
        p = jax.nn.softmax(s, -1).astype(v.dtype)
        return jnp.einsum('bhqk,bhkd->bqhd', p, v)

    out = lax.map(blk, (jnp.arange(nb), qn, qr))
    return out.transpose(1, 0, 2, 3, 4).reshape(B, L, H, D_V)


def mla_sample_attend(q_nope, q_rope, c_kv, k_rope, ckv_past, kr_past, w_uk, w_uv):
    T = q_nope.shape[1]
    scale = (D_NOPE + D_ROPE) ** -0.5
    q_lat = jnp.einsum('bthd,rhd->bthr', q_nope, w_uk)
    s_past = jnp.einsum('bthr,bpr->bhtp', q_lat, ckv_past) + jnp.einsum('bthd,bpd->bhtp', q_rope, kr_past)
    s_new = jnp.einsum('bthr,bur->bhtu', q_lat, c_kv) + jnp.einsum('bthd,bud->bhtu', q_rope, k_rope)
    s_new = jnp.where(jnp.tril(jnp.ones((T, T), dtype=bool)), s_new.astype(F32), -jnp.inf)
    P = ckv_past.shape[1]
    p = jax.nn.softmax(jnp.concatenate([s_past.astype(F32), s_new], -1) * scale, -1).astype(c_kv.dtype)
    o_lat = (jnp.einsum('bhtp,bpr->bthr', p[..., :P], ckv_past)
             + jnp.einsum('bhtu,bur->bthr', p[..., P:], c_kv))
    return jnp.einsum('bthr,rhd->bthd', o_lat, w_uv)


def mixer_layer(x, pos, s_ret, c_ml, n_ml, m_ml, attend,
                g_norm, w_in, b_ig, b_fg, g_ret, g_ml, g_q, g_kv, w_uq, w_out):
    B, L, _ = x.shape
    h = rms_norm(x, g_norm)
    split_at = np.cumsum(IN_WIDTHS)[:-1].tolist()
    (rq, rk, rv, rz, mq, mk, mv, mo, mz, mi, mf, cq, ckv, kr, az) = jnp.split(h @ w_in, split_at, axis=-1)
    heads = lambda t, nh: t.reshape(B, L, nh, -1)
    o_ret, s_ret = retention(rope(heads(rq, H_RET), pos),
                             rope(heads(rk, H_RET), pos) * HEAD_DIM ** -0.5,
                             heads(rv, H_RET), s_ret)
    o_ret = head_norm(o_ret, g_ret) * jax.nn.silu(rz.astype(F32))
    h_ml, c_ml, n_ml, m_ml = mlstm(heads(mq, H_ML), heads(mk, H_ML) * HEAD_DIM ** -0.5, heads(mv, H_ML),
                                   mi.astype(F32) + b_ig, mf.astype(F32) + b_fg, c_ml, n_ml, m_ml)
    o_ml = head_norm(h_ml * jax.nn.sigmoid(heads(mo, H_ML).astype(F32)), g_ml) * jax.nn.silu(mz.astype(F32))
    c_kv = rms_norm(ckv, g_kv)
    k_rope = rope(kr[:, :, None, :], pos)[:, :, 0, :]
    q = jnp.einsum('blr,rhd->blhd', rms_norm(cq, g_q), w_uq)
    q_nope, q_rope = q[..., :D_NOPE], rope(q[..., D_NOPE:], pos)
    o_mla = attend(q_nope, q_rope, c_kv, k_rope) * jax.nn.silu(heads(az, H_MLA))
    o = jnp.concatenate([o_ret.astype(x.dtype), o_ml.astype(x.dtype),
                         o_mla.reshape(B, L, D_MLA).astype(x.dtype)], -1)
    return x + o @ w_out, (s_ret, c_ml, n_ml, m_ml, c_kv, k_rope)


def setup_inputs(seed: int = 0) -> dict:
    key = jax.random.key(seed)
    ks = jax.random.split(key, 24)
    n_pages = PAST_LEN // PAGE_SIZE
    n_used = DEC_BATCH * n_pages
    n_phys = n_used + n_used // 4
    nrm = lambda k, shape, s: jax.random.normal(k, shape, F32) * s
    return {
        "x_prompt": nrm(ks[0], (BATCH, SEQ, D_MODEL), 1.0),
        "x_sample": nrm(ks[1], (DEC_BATCH, DEC_SEQ, D_MODEL), 1.0),
        "state_ret": nrm(ks[2], (DEPTH, DEC_BATCH, H_RET, HEAD_DIM, HEAD_DIM), 0.5),
        "state_mlstm_c": nrm(ks[3], (DEPTH, DEC_BATCH, H_ML, HEAD_DIM, HEAD_DIM), 0.5),
        "state_mlstm_n": nrm(ks[4], (DEPTH, DEC_BATCH, H_ML, HEAD_DIM), 0.5),
        "state_mlstm_m": nrm(ks[5], (DEPTH, DEC_BATCH, H_ML), 1.0),
        "cache_ckv": nrm(ks[6], (DEPTH, n_phys, PAGE_SIZE, KV_RANK), 1.0),
        "cache_krope": nrm(ks[7], (DEPTH, n_phys, PAGE_SIZE, D_ROPE), 1.0),
        "page_table": jax.random.permutation(ks[8], n_phys)[:n_used].reshape(DEC_BATCH, n_pages).astype(jnp.int32),
        "g_norm": 1.0 + nrm(ks[9], (DEPTH, D_MODEL), 0.02),
        "w_in": nrm(ks[10], (DEPTH, D_MODEL, D_IN), D_MODEL ** -0.5),
        "b_ig": nrm(ks[11], (DEPTH, H_ML), 0.1),
        "b_fg": jnp.linspace(3.0, 6.0, H_ML, dtype=F32)[None, :] + nrm(ks[12], (DEPTH, H_ML), 0.1),
        "g_ret": 1.0 + nrm(ks[13], (DEPTH, D_RET), 0.02),
        "g_ml": 1.0 + nrm(ks[14], (DEPTH, D_ML), 0.02),
        "g_q": 1.0 + nrm(ks[15], (DEPTH, Q_RANK), 0.02),
        "g_kv": 1.0 + nrm(ks[16], (DEPTH, KV_RANK), 0.02),
        "w_uq": nrm(ks[17], (DEPTH, Q_RANK, H_MLA, D_NOPE + D_ROPE), Q_RANK ** -0.5),
        "w_uk": nrm(ks[18], (DEPTH, KV_RANK, H_MLA, D_NOPE), KV_RANK ** -0.5),
        "w_uv": nrm(ks[19], (DEPTH, KV_RANK, H_MLA, D_V), KV_RANK ** -0.5),
        "w_out": nrm(ks[20], (DEPTH, D_MIX, D_MODEL), D_MIX ** -0.5),
        "g_final": 1.0 + nrm(ks[21], (D_MODEL,), 0.02),
    }


def reference(x_prompt, x_sample, state_ret, state_mlstm_c, state_mlstm_n, state_mlstm_m,
              cache_ckv, cache_krope, page_table,
              g_norm, w_in, b_ig, b_fg, g_ret, g_ml, g_q, g_kv, w_uq, w_uk, w_uv, w_out, g_final):
    Bp, Lp, _ = x_prompt.shape
    Bs, Ls, _ = x_sample.shape
    pos_p = jnp.arange(Lp)
    pos_s = PAST_LEN + jnp.arange(Ls)
    zeros_ret = jnp.zeros((Bp, H_RET, HEAD_DIM, HEAD_DIM), F32)
    zeros_c = jnp.zeros((Bp, H_ML, HEAD_DIM, HEAD_DIM), F32)
    zeros_n = jnp.zeros((Bp, H_ML, HEAD_DIM), F32)
    zeros_m = jnp.zeros((Bp, H_ML), F32)
    p_out = [[] for _ in range(6)]
    s_out = [[] for _ in range(6)]
    xp, xs = x_prompt, x_sample
    for l in range(DEPTH):
        shared = (g_norm[l], w_in[l], b_ig[l], b_fg[l], g_ret[l], g_ml[l], g_q[l], g_kv[l], w_uq[l], w_out[l])
        attend_p = functools.partial(mla_prompt_attend, w_uk=w_uk[l], w_uv=w_uv[l])
        xp, st_p = mixer_layer(xp, pos_p, zeros_ret, zeros_c, zeros_n, zeros_m, attend_p, *shared)
        ckv_past = cache_ckv[l, page_table].reshape(Bs, -1, KV_RANK)
        kr_past = cache_krope[l, page_table].reshape(Bs, -1, D_ROPE)
        attend_s = functools.partial(mla_sample_attend, ckv_past=ckv_past, kr_past=kr_past,
                                     w_uk=w_uk[l], w_uv=w_uv[l])
        xs, st_s = mixer_layer(xs, pos_s, state_ret[l], state_mlstm_c[l], state_mlstm_n[l], state_mlstm_m[l],
                               attend_s, *shared)
        for j in range(6):
            p_out[j].append(st_p[j])
            s_out[j].append(st_s[j])
    p_ret, p_ml_c, p_ml_n, p_ml_m, p_ckv, p_krope = [jnp.stack(t) for t in p_out]
    s_ret, s_ml_c, s_ml_n, s_ml_m, s_ckv, s_krope = [jnp.stack(t) for t in s_out]
    y_prompt = rms_norm(xp, g_final)
    y_sample = rms_norm(xs, g_final)
    return (y_prompt, y_sample, p_ret, p_ml_c, p_ml_n, p_ml_m, p_ckv, p_krope,
            s_ret, s_ml_c, s_ml_n, s_ml_m, s_ckv, s_krope)
```

```python
import functools

import numpy as np
import jax
import jax.numpy as jnp
from jax import lax
from jax.experimental import pallas as pl
from jax.experimental.pallas import tpu as pltpu

F32 = jnp.float32
BF16 = jnp.bfloat16

HEAD_DIM = 128
H_RET = 4
H_ML = 4
H_MLA = 8
D_RET = H_RET * HEAD_DIM
D_ML = H_ML * HEAD_DIM
D_MLA = H_MLA * HEAD_DIM
Q_RANK = 512
KV_RANK = 256
D_NOPE = 128
D_ROPE = 64
ROPE_BASE = 10000.0
MIX_CHUNK = 128
NORM_EPS = 1e-6
MLSTM_EPS = 1e-6
ATT_SCALE = (D_NOPE + D_ROPE) ** -0.5
QK_SCALE = HEAD_DIM ** -0.5
IN_WIDTHS = (D_RET, D_RET, D_RET, D_RET, D_ML, D_ML, D_ML, D_ML, D_ML, H_ML, H_ML,
             Q_RANK, KV_RANK, D_ROPE, D_MLA)
D_IN = sum(IN_WIDTHS)

LANES = 128
VMEM_LIMIT_BYTES = 56 * 1024 * 1024

COL_RET = 0
COL_ML = 16
COL_CQ = 36
COL_CKV = 40
COL_AZ = 42
COL_GATE = 50
GATE_I_LANE = D_ROPE
GATE_F_LANE = D_ROPE + H_ML
N_IN_PAD = 52 * LANES
RET_LOG_G = [float(np.log1p(-np.exp2(np.float32(-5.0 - h)))) for h in range(H_RET)]


def _cparams(*sem):
    return pltpu.CompilerParams(dimension_semantics=sem, vmem_limit_bytes=VMEM_LIMIT_BYTES)


def _dot(a, b):
    return jnp.dot(a, b, preferred_element_type=F32)


def _dot_nt(a, b):
    return lax.dot_general(a, b, (((1,), (1,)), ((), ())), preferred_element_type=F32)


def _silu(x):
    return x * jax.nn.sigmoid(x)


def _rms(x, g):
    return x * lax.rsqrt(jnp.mean(x * x, axis=-1, keepdims=True) + NORM_EPS) * g


def _head_norm(o, g):
    mu = jnp.mean(o, axis=-1, keepdims=True)
    d = o - mu
    var = jnp.mean(d * d, axis=-1, keepdims=True)
    return d * lax.rsqrt(var + NORM_EPS) * g


def _rope128(x, cos, sin_signed):
    return x * cos + pltpu.roll(x, HEAD_DIM // 2, axis=1) * sin_signed


def _rope64(x, tab):
    q = D_ROPE // 2
    return x * tab[0] + pltpu.roll(x, LANES - q, axis=1) * tab[1] + pltpu.roll(x, q, axis=1) * tab[2]


def _rmsnorm_kernel(x_ref, g_ref, o_ref):
    o_ref[...] = _rms(x_ref[...], g_ref[...]).astype(o_ref.dtype)


def _rmsnorm(x, g, dtype):
    T, D = x.shape
    tm = min(T, 512)
    return pl.pallas_call(
        _rmsnorm_kernel, grid=(T // tm,),
        in_specs=[pl.BlockSpec((tm, D), lambda i: (i, 0)), pl.BlockSpec((1, D), lambda i: (0, 0))],
        out_specs=pl.BlockSpec((tm, D), lambda i: (i, 0)),
        out_shape=jax.ShapeDtypeStruct((T, D), dtype),
        compiler_params=_cparams("parallel"), name="rmsnorm")(x, g.reshape(1, D))


def _matmul_kernel(h_ref, w_ref, o_ref):
    o_ref[...] = _dot(h_ref[...], w_ref[...])


def _in_proj(hn, w):
    T, D = hn.shape
    N = w.shape[1]
    tm = min(T, 1024)
    tn = 4 * LANES
    return pl.pallas_call(
        _matmul_kernel, grid=(T // tm, N // tn),
        in_specs=[pl.BlockSpec((tm, D), lambda i, j: (i, 0)), pl.BlockSpec((D, tn), lambda i, j: (0, j))],
        out_specs=pl.BlockSpec((tm, tn), lambda i, j: (i, j)),
        out_shape=jax.ShapeDtypeStruct((T, N), F32),
        compiler_params=_cparams("parallel", "arbitrary"), name="in_proj")(hn, w)


def _ret_prompt_kernel(q_ref, k_ref, v_ref, z_ref, cs_ref, g_ref, o_ref, s_ref, S_scr):
    ci = pl.program_id(1)
    c = q_ref.shape[0]

    @pl.when(ci == 0)
    def _():
        S_scr[...] = jnp.zeros_like(S_scr)

    cos = cs_ref[0]
    sin = cs_ref[1]
    row = lax.broadcasted_iota(jnp.int32, (c, c), 0)
    col = lax.broadcasted_iota(jnp.int32, (c, c), 1)
    rel = (row - col).astype(F32)
    idx = lax.broadcasted_iota(jnp.int32, (c, 1), 0).astype(F32)
    for h in range(H_RET):
        sl = slice(h * HEAD_DIM, (h + 1) * HEAD_DIM)
        lg = RET_LOG_G[h]
        q = _rope128(q_ref[:, sl], cos, sin)
        k = _rope128(k_ref[:, sl], cos, sin) * QK_SCALE
        vb = v_ref[:, sl].astype(BF16)
        decay = jnp.where(rel >= 0, jnp.exp(lg * jnp.maximum(rel, 0.0)), 0.0)
        q_dec = jnp.exp(lg * (idx + 1.0))
        k_dec = jnp.exp(lg * (c - 1.0 - idx))
        qb = q.astype(BF16)
        S = S_scr[h]
        sc = _dot_nt(qb, k.astype(BF16)) * decay
        o = _dot(sc.astype(BF16), vb) + _dot(qb, S.astype(BF16)) * q_dec
        S_scr[h] = S * float(np.exp(lg * c)) + _dot((k * k_dec).T.astype(BF16), vb)
        y = _head_norm(o, g_ref[:, sl]) * _silu(z_ref[:, sl])
        o_ref[:, sl] = y.astype(o_ref.dtype)

    @pl.when(ci == pl.num_programs(1) - 1)
    def _():
        s_ref[0] = S_scr[...]


def _ret_prompt(proj, cs, g, B, L):
    c = MIX_CHUNK if L % MIX_CHUNK == 0 else L
    nc = L // c
    wb = D_RET // LANES
    col = lambda j: (lambda b, i: (b * nc + i, COL_RET // wb + j))
    return pl.pallas_call(
        _ret_prompt_kernel, grid=(B, nc),
        in_specs=[pl.BlockSpec((c, D_RET), col(0)), pl.BlockSpec((c, D_RET), col(1)),
                  pl.BlockSpec((c, D_RET), col(2)), pl.BlockSpec((c, D_RET), col(3)),
                  pl.BlockSpec((2, c, HEAD_DIM), lambda b, i: (0, i, 0)),
                  pl.BlockSpec((1, D_RET), lambda b, i: (0, 0))],
        out_specs=[pl.BlockSpec((c, D_RET), lambda b, i: (b * nc + i, 0)),
                   pl.BlockSpec((1, H_RET, HEAD_DIM, HEAD_DIM), lambda b, i: (b, 0, 0, 0))],
        out_shape=[jax.ShapeDtypeStruct((B * L, D_RET), BF16),
                   jax.ShapeDtypeStruct((B, H_RET, HEAD_DIM, HEAD_DIM), F32)],
        scratch_shapes=[pltpu.VMEM((H_RET, HEAD_DIM, HEAD_DIM), F32)],
        compiler_params=_cparams("parallel", "arbitrary"), name="ret_prompt")(proj, proj, proj, proj, cs, g)


def _log_sigmoid(x):
    return -(jnp.maximum(-x, 0.0) + jnp.log1p(jnp.exp(-jnp.abs(x))))


def _mlstm_prompt_kernel(q_ref, k_ref, v_ref, og_ref, z_ref, gate_ref, bias_ref, g_ref,
                         o_ref, c_ref, n_ref, m_ref, C_scr, N_scr, M_scr):
    ci = pl.program_id(1)
    c = q_ref.shape[0]

    @pl.when(ci == 0)
    def _():
        C_scr[...] = jnp.zeros_like(C_scr)
        N_scr[...] = jnp.zeros_like(N_scr)
        M_scr[...] = jnp.zeros_like(M_scr)

    gates = gate_ref[...] + bias_ref[...]
    row = lax.broadcasted_iota(jnp.int32, (c, c), 0)
    col = lax.broadcasted_iota(jnp.int32, (c, c), 1)
    causal = col <= row
    eye = col == row
    for h in range(H_ML):
        sl = slice(h * HEAD_DIM, (h + 1) * HEAD_DIM)
        q = q_ref[:, sl]
        k = k_ref[:, sl] * QK_SCALE
        vb = v_ref[:, sl].astype(BF16)
        i_col = gates[:, GATE_I_LANE + h:GATE_I_LANE + h + 1]
        lf_col = _log_sigmoid(gates[:, GATE_F_LANE + h:GATE_F_LANE + h + 1])
        lf_row = jnp.sum(jnp.where(eye, lf_col, 0.0), axis=0, keepdims=True)
        i_row = jnp.sum(jnp.where(eye, i_col, 0.0), axis=0, keepdims=True)
        b_col = jnp.sum(jnp.where(causal, lf_row, 0.0), axis=1, keepdims=True)
        b_row = jnp.sum(jnp.where(row <= col, lf_col, 0.0), axis=0, keepdims=True)
        m_prev = M_scr[h][:, 0:1]
        log_d = jnp.where(causal, b_col - b_row + i_row, -jnp.inf)
        inter = b_col + m_prev
        m_t = jnp.maximum(inter, jnp.max(log_d, axis=1, keepdims=True))
        w_intra = jnp.exp(log_d - m_t)
        w_inter = jnp.exp(inter - m_t)
        qb = q.astype(BF16)
        Cm = C_scr[h]
        nv = N_scr[h]
        s = _dot_nt(qb, k.astype(BF16)) * w_intra
        num = _dot(s.astype(BF16), vb) + w_inter * _dot(qb, Cm.astype(BF16))
        qn = jnp.sum(s, axis=1, keepdims=True) + w_inter * jnp.sum(q * nv, axis=1, keepdims=True)
        hb = num / (jnp.maximum(jnp.abs(qn), jnp.exp(-m_t)) + MLSTM_EPS)
        m_new = m_t[c - 1:c, :]
        b_last = b_col[c - 1:c, :]
        w_k = jnp.exp(b_last - b_col + i_col - m_new)
        dec = jnp.exp(b_last + m_prev - m_new)
        kw = k * w_k
        C_scr[h] = dec * Cm + _dot(kw.T.astype(BF16), vb)
        N_scr[h] = dec * nv + jnp.sum(kw, axis=0, keepdims=True)
        M_scr[h] = jnp.broadcast_to(m_new, (1, HEAD_DIM))
        y = _head_norm(hb * jax.nn.sigmoid(og_ref[:, sl]), g_ref[:, sl]) * _silu(z_ref[:, sl])
        o_ref[:, sl] = y.astype(o_ref.dtype)

    @pl.when(ci == pl.num_programs(1) - 1)
    def _():
        c_ref[0] = C_scr[...]
        n_ref[0] = N_scr[...]
        m_ref[0] = M_scr[...]


def _mlstm_prompt(proj, gate_bias, g, B, L):
    c = MIX_CHUNK if L % MIX_CHUNK == 0 else L
    nc = L // c
    wb = D_ML // LANES
    col = lambda j: (lambda b, i: (b * nc + i, COL_ML // wb + j))
    vec = jax.ShapeDtypeStruct((B, H_ML, 1, HEAD_DIM), F32)
    vec_spec = pl.BlockSpec((1, H_ML, 1, HEAD_DIM), lambda b, i: (b, 0, 0, 0))
    return pl.pallas_call(
        _mlstm_prompt_kernel, grid=(B, nc),
        in_specs=[pl.BlockSpec((c, D_ML), col(0)), pl.BlockSpec((c, D_ML), col(1)),
                  pl.BlockSpec((c, D_ML), col(2)), pl.BlockSpec((c, D_ML), col(3)),
                  pl.BlockSpec((c, D_ML), col(4)),
                  pl.BlockSpec((c, LANES), lambda b, i: (b * nc + i, COL_GATE)),
                  pl.BlockSpec((1, LANES), lambda b, i: (0, 0)),
                  pl.BlockSpec((1, D_ML), lambda b, i: (0, 0))],
        out_specs=[pl.BlockSpec((c, D_ML), lambda b, i: (b * nc + i, 0)),
                   pl.BlockSpec((1, H_ML, HEAD_DIM, HEAD_DIM), lambda b, i: (b, 0, 0, 0)),
                   vec_spec, vec_spec],
        out_shape=[jax.ShapeDtypeStruct((B * L, D_ML), BF16),
                   jax.ShapeDtypeStruct((B, H_ML, HEAD_DIM, HEAD_DIM), F32), vec, vec],
        scratch_shapes=[pltpu.VMEM((H_ML, HEAD_DIM, HEAD_DIM), F32),
                        pltpu.VMEM((H_ML, 1, HEAD_DIM), F32), pltpu.VMEM((H_ML, 1, HEAD_DIM), F32)],
        compiler_params=_cparams("parallel", "arbitrary"),
        name="mlstm_prompt")(proj, proj, proj, proj, proj, proj, gate_bias, g)


def _col_bcast(row):
    return jnp.broadcast_to(row, (HEAD_DIM, HEAD_DIM)).T


def _ret_step_kernel(q_ref, k_ref, v_ref, z_ref, cs_ref, g_ref, s_ref, o_ref, so_ref, o_scr):
    tb = q_ref.shape[0]
    cos = cs_ref[0][0:1, :]
    sin = cs_ref[1][0:1, :]

    for t in range(tb):
        tr = slice(t, t + 1)
        for h in range(H_RET):
            sl = slice(h * HEAD_DIM, (h + 1) * HEAD_DIM)
            gamma = float(np.exp(RET_LOG_G[h]))
            q = _rope128(q_ref[tr, sl], cos, sin)
            k = _rope128(k_ref[tr, sl], cos, sin) * QK_SCALE
            v = v_ref[tr, sl]
            S = s_ref[0, t, h]
            qk = jnp.sum(q * k, axis=1, keepdims=True)
            o = qk * v + gamma * jnp.sum(_col_bcast(q) * S, axis=0, keepdims=True)
            so_ref[t, h] = gamma * S + _col_bcast(k) * v
            o_scr[tr, sl] = o
    for h in range(H_RET):
        sl = slice(h * HEAD_DIM, (h + 1) * HEAD_DIM)
        y = _head_norm(o_scr[:, sl], g_ref[:, sl]) * _silu(z_ref[:, sl])
        o_ref[:, sl] = y.astype(o_ref.dtype)


def _ret_step(proj, cs, g, state, layer):
    T = proj.shape[0]
    tb = min(T, 8)
    wb = D_RET // LANES
    col = lambda j: (lambda i: (i, COL_RET // wb + j))
    return pl.pallas_call(
        _ret_step_kernel, grid=(T // tb,),
        in_specs=[pl.BlockSpec((tb, D_RET), col(0)), pl.BlockSpec((tb, D_RET), col(1)),
                  pl.BlockSpec((tb, D_RET), col(2)), pl.BlockSpec((tb, D_RET), col(3)),
                  pl.BlockSpec((2, tb, HEAD_DIM), lambda i: (0, i, 0)),
                  pl.BlockSpec((1, D_RET), lambda i: (0, 0)),
                  pl.BlockSpec((1, tb, H_RET, HEAD_DIM, HEAD_DIM), lambda i: (layer, i, 0, 0, 0))],
        out_specs=[pl.BlockSpec((tb, D_RET), lambda i: (i, 0)),
                   pl.BlockSpec((tb, H_RET, HEAD_DIM, HEAD_DIM), lambda i: (i, 0, 0, 0))],
        out_shape=[jax.ShapeDtypeStruct((T, D_RET), BF16),
                   jax.ShapeDtypeStruct((T, H_RET, HEAD_DIM, HEAD_DIM), F32)],
        scratch_shapes=[pltpu.VMEM((tb, D_RET), F32)],
        compiler_params=_cparams("parallel"), name="ret_step")(proj, proj, proj, proj, cs, g, state)


def _mlstm_step_kernel(q_ref, k_ref, v_ref, og_ref, z_ref, gate_ref, bias_ref, g_ref, c_ref, n_ref, m_ref,
                       o_ref, co_ref, no_ref, mo_ref, o_scr):
    tb = q_ref.shape[0]

    for t in range(tb):
        tr = slice(t, t + 1)
        gates = gate_ref[tr, :] + bias_ref[...]
        m_all = m_ref[0, tr, :]
        for h in range(H_ML):
            sl = slice(h * HEAD_DIM, (h + 1) * HEAD_DIM)
            q = q_ref[tr, sl]
            k = k_ref[tr, sl] * QK_SCALE
            v = v_ref[tr, sl]
            ib = gates[:, GATE_I_LANE + h:GATE_I_LANE + h + 1]
            lf = _log_sigmoid(gates[:, GATE_F_LANE + h:GATE_F_LANE + h + 1])
            m_prev = m_all[:, h:h + 1]
            Cm = c_ref[0, t, h]
            nv = n_ref[0, t, h:h + 1, :]
            inter = lf + m_prev
            m_t = jnp.maximum(inter, ib)
            w_intra = jnp.exp(ib - m_t)
            w_inter = jnp.exp(inter - m_t)
            s = jnp.sum(q * k, axis=1, keepdims=True) * w_intra
            num = s * v + w_inter * jnp.sum(_col_bcast(q) * Cm, axis=0, keepdims=True)
            qn = s + w_inter * jnp.sum(q * nv, axis=1, keepdims=True)
            hb = num / (jnp.maximum(jnp.abs(qn), jnp.exp(-m_t)) + MLSTM_EPS)
            co_ref[t, h] = w_inter * Cm + _col_bcast(k * w_intra) * v
            no_ref[t, h:h + 1, :] = w_inter * nv + w_intra * k
            mo_ref[tr, sl] = jnp.broadcast_to(m_t, (1, HEAD_DIM))
            o_scr[tr, sl] = hb
    for h in range(H_ML):
        sl = slice(h * HEAD_DIM, (h + 1) * HEAD_DIM)
        y = _head_norm(o_scr[:, sl] * jax.nn.sigmoid(og_ref[:, sl]), g_ref[:, sl]) * _silu(z_ref[:, sl])
        o_ref[:, sl] = y.astype(o_ref.dtype)


def _mlstm_step(proj, gate_bias, g, c_state, n_state, m_state, layer):
    T = proj.shape[0]
    tb = min(T, 8)
    wb = D_ML // LANES
    col = lambda j: (lambda i: (i, COL_ML // wb + j))
    return pl.pallas_call(
        _mlstm_step_kernel, grid=(T // tb,),
        in_specs=[pl.BlockSpec((tb, D_ML), col(0)), pl.BlockSpec((tb, D_ML), col(1)),
                  pl.BlockSpec((tb, D_ML), col(2)), pl.BlockSpec((tb, D_ML), col(3)),
                  pl.BlockSpec((tb, D_ML), col(4)),
                  pl.BlockSpec((tb, LANES), lambda i: (i, COL_GATE)),
                  pl.BlockSpec((1, LANES), lambda i: (0, 0)),
                  pl.BlockSpec((1, D_ML), lambda i: (0, 0)),
                  pl.BlockSpec((1, tb, H_ML, HEAD_DIM, HEAD_DIM), lambda i: (layer, i, 0, 0, 0)),
                  pl.BlockSpec((1, tb, H_ML, HEAD_DIM), lambda i: (layer, i, 0, 0)),
                  pl.BlockSpec((1, tb, H_ML), lambda i: (layer, i, 0))],
        out_specs=[pl.BlockSpec((tb, D_ML), lambda i: (i, 0)),
                   pl.BlockSpec((tb, H_ML, HEAD_DIM, HEAD_DIM), lambda i: (i, 0, 0, 0)),
                   pl.BlockSpec((tb, H_ML, HEAD_DIM), lambda i: (i, 0, 0)),
                   pl.BlockSpec((tb, D_ML), lambda i: (i, 0))],
        out_shape=[jax.ShapeDtypeStruct((T, D_ML), BF16),
                   jax.ShapeDtypeStruct((T, H_ML, HEAD_DIM, HEAD_DIM), F32),
                   jax.ShapeDtypeStruct((T, H_ML, HEAD_DIM), F32),
                   jax.ShapeDtypeStruct((T, D_ML), F32)],
        scratch_shapes=[pltpu.VMEM((tb, D_ML), F32)],
        compiler_params=_cparams("parallel"),
        name="mlstm_step")(proj, proj, proj, proj, proj, proj, gate_bias, g, c_state, n_state, m_state)


def _mla_common(cq_ref, ckv_ref, gate_ref, tab_ref, gq_ref, gkv_ref, wqn_ref, wqr_ref, ckvn_ref, kr_ref):
    tab = tab_ref[...]
    cqn = _rms(cq_ref[...], gq_ref[...]).astype(BF16)
    ckvn = _rms(ckv_ref[...], gkv_ref[...])
    ckvn_ref[...] = ckvn
    k_rope = _rope64(gate_ref[...], tab)
    kr_ref[...] = k_rope[:, :D_ROPE]
    q_nope = _dot(cqn, wqn_ref[...])
    q_rope = _dot(cqn, wqr_ref[...])
    return tab, ckvn, k_rope, q_nope, q_rope


def _mla_prep_prompt_kernel(cq_ref, ckv_ref, gate_ref, tab_ref, gq_ref, gkv_ref, wqn_ref, wqr_ref, wk_ref, wv_ref,
                            ckvn_ref, kr_ref, q_ref, k_ref, v_ref):
    tab, ckvn, k_rope, q_nope, q_rope = _mla_common(cq_ref, ckv_ref, gate_ref, tab_ref, gq_ref, gkv_ref,
                                                    wqn_ref, wqr_ref, ckvn_ref, kr_ref)
    cb = ckvn.astype(BF16)
    k_nope = _dot(cb, wk_ref[...])
    v_ref[...] = _dot(cb, wv_ref[...]).astype(v_ref.dtype)
    krb = k_rope.astype(k_ref.dtype)
    for h in range(H_MLA):
        sl = slice(h * HEAD_DIM, (h + 1) * HEAD_DIM)
        lo = slice(2 * h * LANES, (2 * h + 1) * LANES)
        hi = slice((2 * h + 1) * LANES, (2 * h + 2) * LANES)
        q_ref[:, lo] = (q_nope[:, sl] * ATT_SCALE).astype(q_ref.dtype)
        q_ref[:, hi] = (_rope64(q_rope[:, sl], tab) * ATT_SCALE).astype(q_ref.dtype)
        k_ref[:, lo] = k_nope[:, sl].astype(k_ref.dtype)
        k_ref[:, hi] = krb


def _mla_prep_sample_kernel(cq_ref, ckv_ref, gate_ref, tab_ref, gq_ref, gkv_ref, wqn_ref, wqr_ref, wukt_ref,
                            ckvn_ref, kr_ref, qs_ref):
    tab, ckvn, k_rope, q_nope, q_rope = _mla_common(cq_ref, ckv_ref, gate_ref, tab_ref, gq_ref, gkv_ref,
                                                    wqn_ref, wqr_ref, ckvn_ref, kr_ref)
    for h in range(H_MLA):
        sl = slice(h * HEAD_DIM, (h + 1) * HEAD_DIM)
        q_lat = _dot(q_nope[:, sl].astype(BF16), wukt_ref[h])
        qs_ref[h, :, 0:KV_RANK] = q_lat * ATT_SCALE
        qs_ref[h, :, KV_RANK:KV_RANK + LANES] = _rope64(q_rope[:, sl], tab) * ATT_SCALE


def _mla_prep(proj, tab, gq, gkv, wqn, wqr, extra_w, sample):
    T = proj.shape[0]
    tm = min(T, 512)
    tab_blocks = tab.shape[1] // tm
    full = lambda a: pl.BlockSpec(a.shape, lambda i: (0,) * a.ndim)
    in_specs = [pl.BlockSpec((tm, Q_RANK), lambda i: (i, COL_CQ * LANES // Q_RANK)),
                pl.BlockSpec((tm, KV_RANK), lambda i: (i, COL_CKV * LANES // KV_RANK)),
                pl.BlockSpec((tm, LANES), lambda i: (i, COL_GATE)),
                pl.BlockSpec((3, tm, LANES), lambda i: (0, i % tab_blocks, 0)),
                full(gq), full(gkv), full(wqn), full(wqr)] + [full(w) for w in extra_w]
    out_specs = [pl.BlockSpec((tm, KV_RANK), lambda i: (i, 0)), pl.BlockSpec((tm, D_ROPE), lambda i: (i, 0))]
    out_shape = [jax.ShapeDtypeStruct((T, KV_RANK), F32), jax.ShapeDtypeStruct((T, D_ROPE), F32)]
    if sample:
        kern = _mla_prep_sample_kernel
        out_specs.append(pl.BlockSpec((H_MLA, tm, KV_RANK + LANES), lambda i: (0, i, 0)))
        out_shape.append(jax.ShapeDtypeStruct((H_MLA, T, KV_RANK + LANES), F32))
    else:
        kern = _mla_prep_prompt_kernel
        out_specs += [pl.BlockSpec((tm, 2 * D_MLA), lambda i: (i, 0)), pl.BlockSpec((tm, 2 * D_MLA), lambda i: (i, 0)),
                      pl.BlockSpec((tm, D_MLA), lambda i: (i, 0))]
        out_shape += [jax.ShapeDtypeStruct((T, 2 * D_MLA), BF16), jax.ShapeDtypeStruct((T, 2 * D_MLA), BF16),
                      jax.ShapeDtypeStruct((T, D_MLA), BF16)]
    return pl.pallas_call(
        kern, grid=(T // tm,), in_specs=in_specs, out_specs=out_specs, out_shape=out_shape,
        compiler_params=_cparams("parallel"),
        name="mla_prep_sample" if sample else "mla_prep_prompt")(proj, proj, proj, tab, gq, gkv, wqn, wqr, *extra_w)


def _attn_prompt_kernel(q_ref, k_ref, v_ref, z_ref, o_ref, *, tq):
    L = q_ref.shape[0]
    row = lax.broadcasted_iota(jnp.int32, (tq, tq), 0)
    col = lax.broadcasted_iota(jnp.int32, (tq, tq), 1)
    for i in range(L // tq):
        cur = slice(i * tq, (i + 1) * tq)
        q = q_ref[cur, :]
        sd = jnp.where(col <= row, _dot_nt(q, k_ref[cur, :]), -jnp.inf)
        m = jnp.max(sd, axis=1, keepdims=True)
        if i > 0:
            so = _dot_nt(q, k_ref[0:i * tq, :])
            m = jnp.maximum(m, jnp.max(so, axis=1, keepdims=True))
            po = jnp.exp(so - m)
            l = jnp.sum(po, axis=1, keepdims=True)
            acc = _dot(po.astype(BF16), v_ref[0:i * tq, :])
        pd = jnp.exp(sd - m)
        if i > 0:
            l = l + jnp.sum(pd, axis=1, keepdims=True)
            acc = acc + _dot(pd.astype(BF16), v_ref[cur, :])
        else:
            l = jnp.sum(pd, axis=1, keepdims=True)
            acc = _dot(pd.astype(BF16), v_ref[cur, :])
        o_ref[cur, :] = (acc / l * _silu(z_ref[cur, :])).astype(o_ref.dtype)


def _attn_prompt(q, k, v, proj, B, L):
    tq = min(L, 512)
    return pl.pallas_call(
        functools.partial(_attn_prompt_kernel, tq=tq), grid=(B, H_MLA),
        in_specs=[pl.BlockSpec((L, 2 * LANES), lambda b, h: (b, h)),
                  pl.BlockSpec((L, 2 * LANES), lambda b, h: (b, h)),
                  pl.BlockSpec((L, HEAD_DIM), lambda b, h: (b, h)),
                  pl.BlockSpec((L, HEAD_DIM), lambda b, h: (b, COL_AZ + h))],
        out_specs=pl.BlockSpec((L, HEAD_DIM), lambda b, h: (b, h)),
        out_shape=jax.ShapeDtypeStruct((B * L, D_MLA), BF16),
        compiler_params=_cparams("parallel", "parallel"), name="attn_prompt")(q, k, v, proj)


def _decode_kernel(pt_ref, qs_ref, cn_ref, kn_ref, ckv_hbm, kr_hbm, o_ref, ckv_buf, kr_buf, sem,
                   *, layer, n_pages, page, chunk):
    b = pl.program_id(0)
    nb = pl.num_programs(0)
    slot = b % 2

    def page_copies(seq, sl, p):
        pg = pt_ref[seq, p]
        dst = pl.ds(pl.multiple_of(p * page, page), page)
        return (pltpu.make_async_copy(ckv_hbm.at[layer, pg], ckv_buf.at[sl, dst, :], sem.at[0, sl]),
                pltpu.make_async_copy(kr_hbm.at[layer, pg], kr_buf.at[sl, dst, :], sem.at[1, sl]))

    def start_fetch(seq, sl):
        def body(p, carry):
            for cp in page_copies(seq, sl, p):
                cp.start()
            return carry
        lax.fori_loop(0, n_pages, body, 0)

    def wait_fetch(seq, sl):
        def body(p, carry):
            for cp in page_copies(seq, sl, p):
                cp.wait()
            return carry
        lax.fori_loop(0, n_pages, body, 0)

    @pl.when(b == 0)
    def _():
        start_fetch(0, 0)

    @pl.when(b + 1 < nb)
    def _():
        start_fetch(b + 1, 1 - slot)

    wait_fetch(b, slot)

    q = qs_ref[0]
    q16 = jnp.concatenate([q, jnp.zeros_like(q)], axis=0).astype(BF16)
    ql = q16[:, 0:KV_RANK]
    qr = q16[:, KV_RANK:KV_RANK + D_ROPE]
    nh = q16.shape[0]
    m = jnp.full((nh, 1), -jnp.inf, F32)
    l = jnp.zeros((nh, 1), F32)
    acc = jnp.zeros((nh, KV_RANK), F32)
    P = n_pages * page
    for c in range(P // chunk):
        rows = pl.ds(c * chunk, chunk)
        kc = ckv_buf[slot, rows, :].astype(BF16)
        rc = kr_buf[slot, rows, :].astype(BF16)
        s = _dot_nt(ql, kc) + _dot_nt(qr, rc)
        m_new = jnp.maximum(m, jnp.max(s, axis=1, keepdims=True))
        alpha = jnp.exp(m - m_new)
        p = jnp.exp(s - m_new)
        l = alpha * l + jnp.sum(p, axis=1, keepdims=True)
        acc = alpha * acc + _dot(p.astype(BF16), kc)
        m = m_new
    cn = cn_ref[0].astype(BF16).astype(F32)
    kn = kn_ref[0].astype(BF16).astype(F32)
    s_new = (jnp.sum(ql.astype(F32) * cn, axis=1, keepdims=True)
             + jnp.sum(qr.astype(F32) * kn, axis=1, keepdims=True))
    m_new = jnp.maximum(m, s_new)
    alpha = jnp.exp(m - m_new)
    p_new = jnp.exp(s_new - m_new)
    l = alpha * l + p_new
    acc = alpha * acc + p_new.astype(BF16).astype(F32) * cn
    o_ref[0] = (acc / l)[0:H_MLA, :]


def _decode(page_table, qs, ckv_new, kr_new, cache_ckv, cache_krope, layer):
    DB, n_pages = page_table.shape
    page = cache_ckv.shape[2]
    P = n_pages * page
    chunk = min(P, 2048)
    kern = functools.partial(_decode_kernel, layer=layer, n_pages=n_pages, page=page, chunk=chunk)
    grid_spec = pltpu.PrefetchScalarGridSpec(
        num_scalar_prefetch=1, grid=(DB,),
        in_specs=[pl.BlockSpec((1, H_MLA, KV_RANK + LANES), lambda b, pt: (b, 0, 0)),
                  pl.BlockSpec((1, 1, KV_RANK), lambda b, pt: (b, 0, 0)),
                  pl.BlockSpec((1, 1, D_ROPE), lambda b, pt: (b, 0, 0)),
                  pl.BlockSpec(memory_space=pl.ANY), pl.BlockSpec(memory_space=pl.ANY)],
        out_specs=pl.BlockSpec((1, H_MLA, KV_RANK), lambda b, pt: (b, 0, 0)),
        scratch_shapes=[pltpu.VMEM((2, P, KV_RANK), F32), pltpu.VMEM((2, P, D_ROPE), F32),
                        pltpu.SemaphoreType.DMA((2, 2))])
    return pl.pallas_call(
        kern, grid_spec=grid_spec, out_shape=jax.ShapeDtypeStruct((DB, H_MLA, KV_RANK), F32),
        compiler_params=_cparams("arbitrary"), name="mla_decode")(
            page_table, qs, ckv_new.reshape(DB, 1, KV_RANK), kr_new.reshape(DB, 1, D_ROPE), cache_ckv, cache_krope)


def _mla_out_kernel(ol_ref, w_ref, z_ref, o_ref):
    o = _dot(ol_ref[0].astype(BF16), w_ref[0])
    o_ref[...] = (o * _silu(z_ref[...])).astype(o_ref.dtype)


def _mla_out(o_lat_h, wuv, proj):
    T = proj.shape[0]
    return pl.pallas_call(
        _mla_out_kernel, grid=(H_MLA,),
        in_specs=[pl.BlockSpec((1, T, KV_RANK), lambda h: (h, 0, 0)),
                  pl.BlockSpec((1, KV_RANK, HEAD_DIM), lambda h: (h, 0, 0)),
                  pl.BlockSpec((T, HEAD_DIM), lambda h: (0, COL_AZ + h))],
        out_specs=pl.BlockSpec((T, HEAD_DIM), lambda h: (0, h)),
        out_shape=jax.ShapeDtypeStruct((T, D_MLA), BF16),
        compiler_params=_cparams("parallel"), name="mla_out")(o_lat_h, wuv, proj)


def _out_proj_kernel(or_ref, om_ref, oa_ref, x_ref, w_ref, g_ref, *out_refs, last):
    x = (x_ref[...] + _dot(or_ref[...], w_ref[0:D_RET, :]) + _dot(om_ref[...], w_ref[D_RET:D_RET + D_ML, :])
         + _dot(oa_ref[...], w_ref[D_RET + D_ML:, :]))
    hn = _rms(x, g_ref[...])
    if last:
        out_refs[0][...] = hn
    else:
        out_refs[0][...] = x
        out_refs[1][...] = hn.astype(out_refs[1].dtype)


def _out_proj(o_ret, o_ml, o_mla, x, w, g, last):
    T, D = x.shape
    tm = min(T, 256)
    row = lambda n: pl.BlockSpec((tm, n), lambda i: (i, 0))
    if last:
        out_specs, out_shape = [row(D)], [jax.ShapeDtypeStruct((T, D), F32)]
    else:
        out_specs = [row(D), row(D)]
        out_shape = [jax.ShapeDtypeStruct((T, D), F32), jax.ShapeDtypeStruct((T, D), BF16)]
    return pl.pallas_call(
        functools.partial(_out_proj_kernel, last=last), grid=(T // tm,),
        in_specs=[row(D_RET), row(D_ML), row(D_MLA), row(D),
                  pl.BlockSpec(w.shape, lambda i: (0, 0)), pl.BlockSpec((1, D), lambda i: (0, 0))],
        out_specs=out_specs, out_shape=out_shape,
        compiler_params=_cparams("parallel"), name="out_proj")(o_ret, o_ml, o_mla, x, w, g.reshape(1, D))


def _rope_tables(pos):
    pos = pos.astype(F32)[:, None]

    def cs(half):
        inv = ROPE_BASE ** (-jnp.arange(half, dtype=F32) / half)
        ang = pos * inv[None, :]
        return jnp.cos(ang), jnp.sin(ang)

    c64, s64 = cs(HEAD_DIM // 2)
    tab128 = jnp.stack([jnp.concatenate([c64, c64], -1), jnp.concatenate([-s64, s64], -1)])
    c32, s32 = cs(D_ROPE // 2)
    z32 = jnp.zeros_like(c32)
    tab64 = jnp.stack([jnp.concatenate([c32, c32, z32, z32], -1),
                       jnp.concatenate([-s32, z32, z32, z32], -1),
                       jnp.concatenate([z32, s32, z32, z32], -1)])
    return tab128, tab64


def _layout_w_in(w_in):
    o = np.cumsum((0,) + IN_WIDTHS)
    seg = lambda a, b: w_in[:, :, o[a]:o[b]]
    pad = jnp.zeros(w_in.shape[:2] + (N_IN_PAD - D_IN,), w_in.dtype)
    return jnp.concatenate([seg(0, 9), seg(11, 12), seg(12, 13), seg(14, 15), seg(13, 14), seg(9, 11), pad],
                           axis=-1).astype(BF16)


def kernel(x_prompt, x_sample, state_ret, state_mlstm_c, state_mlstm_n, state_mlstm_m, cache_ckv, cache_krope,
           page_table, g_norm, w_in, b_ig, b_fg, g_ret, g_ml, g_q, g_kv, w_uq, w_uk, w_uv, w_out, g_final):
    B, L, D = x_prompt.shape
    DB, Ls, _ = x_sample.shape
    assert Ls == 1, "the sample group is one new token per sequence"
    depth = w_in.shape[0]
    past_len = page_table.shape[1] * cache_ckv.shape[2]

    tab128_p, tab64_p = _rope_tables(jnp.arange(L))
    tab128_s, tab64_s = _rope_tables(past_len + jnp.arange(Ls))
    tab128_s = jnp.broadcast_to(tab128_s, (2, DB, HEAD_DIM))
    tab64_s = jnp.broadcast_to(tab64_s, (3, DB, LANES))
    w_in_b = _layout_w_in(w_in)
    w_out_b = w_out.astype(BF16)
    wqn = w_uq[..., :D_NOPE].reshape(depth, Q_RANK, D_MLA).astype(BF16)
    wqr = jnp.concatenate([w_uq[..., D_NOPE:], jnp.zeros(w_uq.shape[:3] + (LANES - D_ROPE,), w_uq.dtype)],
                          -1).reshape(depth, Q_RANK, D_MLA).astype(BF16)
    wk = w_uk.reshape(depth, KV_RANK, D_MLA).astype(BF16)
    wv = w_uv.reshape(depth, KV_RANK, D_MLA).astype(BF16)
    wukt = jnp.transpose(w_uk, (0, 2, 3, 1)).astype(BF16)
    wuv_h = jnp.transpose(w_uv, (0, 2, 1, 3)).astype(BF16)
    zeros_gate = jnp.zeros((depth, LANES - D_ROPE - 2 * H_ML), F32)
    gate_bias = jnp.concatenate([jnp.zeros((depth, D_ROPE), F32), b_ig, b_fg, zeros_gate], -1)

    xp = x_prompt.reshape(B * L, D)
    xs = x_sample.reshape(DB, D)
    hp = _rmsnorm(xp, g_norm[0], BF16)
    hs = _rmsnorm(xs, g_norm[0], BF16)
    p_out = [[] for _ in range(6)]
    s_out = [[] for _ in range(6)]
    for l in range(depth):
        last = l == depth - 1
        g_next = g_final if last else g_norm[l + 1]
        gq, gkv = g_q[l].reshape(1, Q_RANK), g_kv[l].reshape(1, KV_RANK)
        gr, gm = g_ret[l].reshape(1, D_RET), g_ml[l].reshape(1, D_ML)
        gb = gate_bias[l].reshape(1, LANES)

        proj = _in_proj(hp, w_in_b[l])
        o_ret, st_ret = _ret_prompt(proj, tab128_p, gr, B, L)
        o_ml, st_c, st_n, st_m = _mlstm_prompt(proj, gb, gm, B, L)
        ckvn, krope, q, k, v = _mla_prep(proj, tab64_p, gq, gkv, wqn[l], wqr[l], (wk[l], wv[l]), sample=False)
        o_mla = _attn_prompt(q, k, v, proj, B, L)
        res = _out_proj(o_ret, o_ml, o_mla, xp, w_out_b[l], g_next, last)
        if last:
            y_prompt = res[0]
        else:
            xp, hp = res
        for j, t in enumerate((st_ret, st_c, st_n[:, :, 0, :], st_m[:, :, 0, 0],
                               ckvn.reshape(B, L, KV_RANK), krope.reshape(B, L, D_ROPE))):
            p_out[j].append(t)

        proj_s = _in_proj(hs, w_in_b[l])
        o_ret, st_ret = _ret_step(proj_s, tab128_s, gr, state_ret, l)
        o_ml, st_c, st_n, st_m = _mlstm_step(proj_s, gb, gm, state_mlstm_c, state_mlstm_n, state_mlstm_m, l)
        ckvn, krope, qs = _mla_prep(proj_s, tab64_s, gq, gkv, wqn[l], wqr[l], (wukt[l],), sample=True)
        o_lat = _decode(page_table, jnp.transpose(qs, (1, 0, 2)), ckvn, krope, cache_ckv, cache_krope, l)
        o_mla = _mla_out(jnp.transpose(o_lat, (1, 0, 2)), wuv_h[l], proj_s)
        res = _out_proj(o_ret, o_ml, o_mla, xs, w_out_b[l], g_next, last)
        if last:
            y_sample = res[0]
        else:
            xs, hs = res
        for j, t in enumerate((st_ret, st_c, st_n, st_m.reshape(DB, H_ML, HEAD_DIM)[:, :, 0],
                               ckvn.reshape(DB, Ls, KV_RANK), krope.reshape(DB, Ls, D_ROPE))):
            s_out[j].append(t)

    return (y_prompt.reshape(B, L, D), y_sample.reshape(DB, Ls, D),
            *[jnp.stack(t) for t in p_out], *[jnp.stack(t) for t in s_out])
```

```python
import functools

import numpy as np
import jax
import jax.numpy as jnp
from jax import lax
from jax.experimental import pallas as pl
from jax.experimental.pallas import tpu as pltpu

F32 = jnp.float32
BF16 = jnp.bfloat16

HEAD_DIM = 128
H_RET = 4
H_ML = 4
H_MLA = 8
D_RET = H_RET * HEAD_DIM
D_ML = H_ML * HEAD_DIM
D_MLA = H_MLA * HEAD_DIM
Q_RANK = 512
KV_RANK = 256
D_NOPE = 128
D_ROPE = 64
ROPE_BASE = 10000.0
MIX_CHUNK = 128
NORM_EPS = 1e-6
MLSTM_EPS = 1e-6
ATT_SCALE = (D_NOPE + D_ROPE) ** -0.5
QK_SCALE = HEAD_DIM ** -0.5
IN_WIDTHS = (D_RET, D_RET, D_RET, D_RET, D_ML, D_ML, D_ML, D_ML, D_ML, H_ML, H_ML,
             Q_RANK, KV_RANK, D_ROPE, D_MLA)
D_IN = sum(IN_WIDTHS)

LANES = 128
VMEM_LIMIT_BYTES = 56 * 1024 * 1024

COL_RET = 0
COL_ML = 16
COL_CQ = 36
COL_CKV = 40
COL_AZ = 42
COL_GATE = 50
GATE_I_LANE = D_ROPE
GATE_F_LANE = D_ROPE + H_ML
N_IN_PAD = 52 * LANES
RET_LOG_G = [float(np.log1p(-np.exp2(np.float32(-5.0 - h)))) for h in range(H_RET)]


def _cparams(*sem):
    return pltpu.CompilerParams(dimension_semantics=sem, vmem_limit_bytes=VMEM_LIMIT_BYTES)


def _dot(a, b):
    return jnp.dot(a, b, preferred_element_type=F32)


def _dot_nt(a, b):
    return lax.dot_general(a, b, (((1,), (1,)), ((), ())), preferred_element_type=F32)


def _silu(x):
    return x * jax.nn.sigmoid(x)


def _rms(x, g):
    return x * lax.rsqrt(jnp.mean(x * x, axis=-1, keepdims=True) + NORM_EPS) * g


def _head_norm(o, g):
    mu = jnp.mean(o, axis=-1, keepdims=True)
    d = o - mu
    var = jnp.mean(d * d, axis=-1, keepdims=True)
    return d * lax.rsqrt(var + NORM_EPS) * g


def _rope128(x, cos, sin_signed):
    return x * cos + pltpu.roll(x, HEAD_DIM // 2, axis=1) * sin_signed


def _rope64(x, tab):
    q = D_ROPE // 2
    return x * tab[0] + pltpu.roll(x, LANES - q, axis=1) * tab[1] + pltpu.roll(x, q, axis=1) * tab[2]


def _rmsnorm_kernel(x_ref, g_ref, o_ref):
    o_ref[...] = _rms(x_ref[...], g_ref[...]).astype(o_ref.dtype)


def _rmsnorm(x, g, dtype):
    T, D = x.shape
    tm = min(T, 512)
    return pl.pallas_call(
        _rmsnorm_kernel, grid=(T // tm,),
        in_specs=[pl.BlockSpec((tm, D), lambda i: (i, 0)), pl.BlockSpec((1, D), lambda i: (0, 0))],
        out_specs=pl.BlockSpec((tm, D), lambda i: (i, 0)),
        out_shape=jax.ShapeDtypeStruct((T, D), dtype),
        compiler_params=_cparams("parallel"), name="rmsnorm")(x, g.reshape(1, D))


def _matmul_nt_kernel(h_ref, wt_ref, o_ref):
    o_ref[...] = _dot_nt(h_ref[...], wt_ref[...])


def _in_proj(hn, wt, layer):
    T, D = hn.shape
    N = wt.shape[1]
    tm = min(T, 1024)
    tn = 4 * LANES
    return pl.pallas_call(
        _matmul_nt_kernel, grid=(T // tm, N // tn),
        in_specs=[pl.BlockSpec((tm, D), lambda i, j: (i, 0)),
                  pl.BlockSpec((None, tn, D), lambda i, j: (layer, j, 0))],
        out_specs=pl.BlockSpec((tm, tn), lambda i, j: (i, j)),
        out_shape=jax.ShapeDtypeStruct((T, N), F32),
        compiler_params=_cparams("parallel", "arbitrary"), name="in_proj")(hn, wt)


def _ret_prompt_kernel(q_ref, k_ref, v_ref, z_ref, cs_ref, g_ref, o_ref, s_ref, S_scr):
    ci = pl.program_id(0)
    B, c = q_ref.shape[0], q_ref.shape[1]

    @pl.when(ci == 0)
    def _():
        S_scr[...] = jnp.zeros_like(S_scr)

    cos = cs_ref[0]
    sin = cs_ref[1]
    row = lax.broadcasted_iota(jnp.int32, (c, c), 0)
    col = lax.broadcasted_iota(jnp.int32, (c, c), 1)
    rel = (row - col).astype(F32)
    idx = lax.broadcasted_iota(jnp.int32, (c, 1), 0).astype(F32)
    for h in range(H_RET):
        sl = slice(h * HEAD_DIM, (h + 1) * HEAD_DIM)
        lg = RET_LOG_G[h]
        decay = jnp.where(rel >= 0, jnp.exp(lg * jnp.maximum(rel, 0.0)), 0.0)
        q_dec = jnp.exp(lg * (idx + 1.0))
        k_dec = jnp.exp(lg * (c - 1.0 - idx))
        for b in range(B):
            q = _rope128(q_ref[b, :, sl], cos, sin)
            k = _rope128(k_ref[b, :, sl], cos, sin) * QK_SCALE
            vb = v_ref[b, :, sl].astype(BF16)
            qb = q.astype(BF16)
            S = S_scr[b, h]
            sc = _dot_nt(qb, k.astype(BF16)) * decay
            o = _dot(sc.astype(BF16), vb) + _dot(qb, S.astype(BF16)) * q_dec
            S_scr[b, h] = S * float(np.exp(lg * c)) + _dot((k * k_dec).T.astype(BF16), vb)
            y = _head_norm(o, g_ref[:, sl]) * _silu(z_ref[b, :, sl])
            o_ref[b, :, sl] = y.astype(o_ref.dtype)

    @pl.when(ci == pl.num_programs(0) - 1)
    def _():
        s_ref[...] = S_scr[...]


def _ret_prompt(proj, cs, g, B, L):
    c = MIX_CHUNK if L % MIX_CHUNK == 0 else L
    wb = D_RET // LANES
    col = lambda j: (lambda i: (0, i, COL_RET // wb + j))
    proj3 = proj.reshape(B, L, proj.shape[1])
    o, st = pl.pallas_call(
        _ret_prompt_kernel, grid=(L // c,),
        in_specs=[pl.BlockSpec((B, c, D_RET), col(0)), pl.BlockSpec((B, c, D_RET), col(1)),
                  pl.BlockSpec((B, c, D_RET), col(2)), pl.BlockSpec((B, c, D_RET), col(3)),
                  pl.BlockSpec((2, c, HEAD_DIM), lambda i: (0, i, 0)),
                  pl.BlockSpec((1, D_RET), lambda i: (0, 0))],
        out_specs=[pl.BlockSpec((B, c, D_RET), lambda i: (0, i, 0)),
                   pl.BlockSpec((B, H_RET, HEAD_DIM, HEAD_DIM), lambda i: (0, 0, 0, 0))],
        out_shape=[jax.ShapeDtypeStruct((B, L, D_RET), BF16),
                   jax.ShapeDtypeStruct((B, H_RET, HEAD_DIM, HEAD_DIM), F32)],
        scratch_shapes=[pltpu.VMEM((B, H_RET, HEAD_DIM, HEAD_DIM), F32)],
        compiler_params=_cparams("arbitrary"), name="ret_prompt")(proj3, proj3, proj3, proj3, cs, g)
    return o.reshape(B * L, D_RET), st


def _log_sigmoid(x):
    return -(jnp.maximum(-x, 0.0) + jnp.log1p(jnp.exp(-jnp.abs(x))))


def _mlstm_prompt_kernel(q_ref, k_ref, v_ref, og_ref, z_ref, gate_ref, bias_ref, g_ref,
                         o_ref, c_ref, n_ref, m_ref, C_scr, M_scr):
    ci = pl.program_id(0)
    B, c = q_ref.shape[0], q_ref.shape[1]

    @pl.when(ci == 0)
    def _():
        C_scr[...] = jnp.zeros_like(C_scr)
        M_scr[...] = jnp.zeros_like(M_scr)

    row = lax.broadcasted_iota(jnp.int32, (c, c), 0)
    col = lax.broadcasted_iota(jnp.int32, (c, c), 1)
    causal = col <= row
    eye = col == row
    ones = jnp.ones((c, HEAD_DIM), BF16)
    for b in range(B):
        gates = gate_ref[b] + bias_ref[...]
        for h in range(H_ML):
            sl = slice(h * HEAD_DIM, (h + 1) * HEAD_DIM)
            q = q_ref[b, :, sl]
            k = k_ref[b, :, sl] * QK_SCALE
            vb1 = jnp.concatenate([v_ref[b, :, sl].astype(BF16), ones], axis=1)
            i_col = gates[:, GATE_I_LANE + h:GATE_I_LANE + h + 1]
            lf_col = _log_sigmoid(gates[:, GATE_F_LANE + h:GATE_F_LANE + h + 1])
            lf_row = jnp.sum(jnp.where(eye, lf_col, 0.0), axis=0, keepdims=True)
            i_row = jnp.sum(jnp.where(eye, i_col, 0.0), axis=0, keepdims=True)
            b_col = jnp.sum(jnp.where(causal, lf_row, 0.0), axis=1, keepdims=True)
            b_row = jnp.sum(jnp.where(row <= col, lf_col, 0.0), axis=0, keepdims=True)
            m_prev = M_scr[b, h][:, 0:1]
            log_d = jnp.where(causal, b_col - b_row + i_row, -jnp.inf)
            inter = b_col + m_prev
            m_t = jnp.maximum(inter, jnp.max(log_d, axis=1, keepdims=True))
            w_intra = jnp.exp(log_d - m_t)
            w_inter = jnp.exp(inter - m_t)
            qb = q.astype(BF16)
            C1 = C_scr[b, h]
            s = _dot_nt(qb, k.astype(BF16)) * w_intra
            a = _dot(s.astype(BF16), vb1) + w_inter * _dot(qb, C1.astype(BF16))
            hb = a[:, :HEAD_DIM] / (jnp.maximum(jnp.abs(a[:, HEAD_DIM:]), jnp.exp(-m_t)) + MLSTM_EPS)
            m_new = m_t[c - 1:c, :]
            b_last = b_col[c - 1:c, :]
            w_k = jnp.exp(b_last - b_col + i_col - m_new)
            dec = jnp.exp(b_last + m_prev - m_new)
            C_scr[b, h] = dec * C1 + _dot((k * w_k).T.astype(BF16), vb1)
            M_scr[b, h] = jnp.broadcast_to(m_new, (1, HEAD_DIM))
            y = _head_norm(hb * jax.nn.sigmoid(og_ref[b, :, sl]), g_ref[:, sl]) * _silu(z_ref[b, :, sl])
            o_ref[b, :, sl] = y.astype(o_ref.dtype)

    @pl.when(ci == pl.num_programs(0) - 1)
    def _():
        m_ref[...] = M_scr[...]
        for b in range(B):
            for h in range(H_ML):
                C1 = C_scr[b, h]
                c_ref[b, h] = C1[:, :HEAD_DIM]
                n_ref[b, h] = C1[:, HEAD_DIM:].T[0:1, :]


def _mlstm_prompt(proj, gate_bias, g, B, L):
    c = MIX_CHUNK if L % MIX_CHUNK == 0 else L
    wb = D_ML // LANES
    col = lambda j: (lambda i: (0, i, COL_ML // wb + j))
    proj3 = proj.reshape(B, L, proj.shape[1])
    vec = jax.ShapeDtypeStruct((B, H_ML, 1, HEAD_DIM), F32)
    vec_spec = pl.BlockSpec((B, H_ML, 1, HEAD_DIM), lambda i: (0, 0, 0, 0))
    o, st_c, st_n, st_m = pl.pallas_call(
        _mlstm_prompt_kernel, grid=(L // c,),
        in_specs=[pl.BlockSpec((B, c, D_ML), col(0)), pl.BlockSpec((B, c, D_ML), col(1)),
                  pl.BlockSpec((B, c, D_ML), col(2)), pl.BlockSpec((B, c, D_ML), col(3)),
                  pl.BlockSpec((B, c, D_ML), col(4)),
                  pl.BlockSpec((B, c, LANES), lambda i: (0, i, COL_GATE)),
                  pl.BlockSpec((1, LANES), lambda i: (0, 0)),
                  pl.BlockSpec((1, D_ML), lambda i: (0, 0))],
        out_specs=[pl.BlockSpec((B, c, D_ML), lambda i: (0, i, 0)),
                   pl.BlockSpec((B, H_ML, HEAD_DIM, HEAD_DIM), lambda i: (0, 0, 0, 0)),
                   vec_spec, vec_spec],
        out_shape=[jax.ShapeDtypeStruct((B, L, D_ML), BF16),
                   jax.ShapeDtypeStruct((B, H_ML, HEAD_DIM, HEAD_DIM), F32), vec, vec],
        scratch_shapes=[pltpu.VMEM((B, H_ML, HEAD_DIM, 2 * HEAD_DIM), F32),
                        pltpu.VMEM((B, H_ML, 1, HEAD_DIM), F32)],
        compiler_params=_cparams("arbitrary"),
        name="mlstm_prompt")(proj3, proj3, proj3, proj3, proj3, proj3, gate_bias, g)
    return o.reshape(B * L, D_ML), st_c, st_n, st_m


STEP_TOKENS = 16


def _outer_t(kb, vm):
    return lax.dot_general(kb, vm, (((0,), (0,)), ((), ())), preferred_element_type=F32)


def _ret_step_kernel(q_ref, k_ref, v_ref, z_ref, cs_ref, g_ref, s_ref, *rest):
    o_ref, so_ref = rest[-2:]
    tb = q_ref.shape[0]
    cos = cs_ref[0]
    sin = cs_ref[1]
    tok = lax.broadcasted_iota(jnp.int32, (tb, HEAD_DIM), 0)
    for h in range(H_RET):
        sl = slice(h * HEAD_DIM, (h + 1) * HEAD_DIM)
        gamma = float(np.exp(RET_LOG_G[h]))
        q = _rope128(q_ref[:, sl], cos, sin)
        k = _rope128(k_ref[:, sl], cos, sin) * QK_SCALE
        v = v_ref[:, sl]
        qb = q.astype(BF16)
        kb = k.astype(BF16)
        ro = jnp.zeros((tb, HEAD_DIM), F32)
        for t in range(tb):
            S = s_ref[0, t, h]
            ro = jnp.where(tok == t, _dot(qb, S.astype(BF16)), ro)
            vm = jnp.where(tok == t, v, 0.0).astype(BF16)
            so_ref[0, t, h] = gamma * S + _outer_t(kb, vm)
        o = jnp.sum(q * k, axis=1, keepdims=True) * v + gamma * ro
        y = _head_norm(o, g_ref[:, sl]) * _silu(z_ref[:, sl])
        o_ref[:, sl] = y.astype(o_ref.dtype)


def _stacked_state_specs(prev, depth, T, tb, H, layer):
    shape = jax.ShapeDtypeStruct((depth, T, H, HEAD_DIM, HEAD_DIM), F32)
    spec = pl.BlockSpec((1, tb, H, HEAD_DIM, HEAD_DIM), lambda i: (layer, i, 0, 0, 0))
    if prev is None:
        prev = jnp.zeros(shape.shape, shape.dtype)
    return shape, spec, [pl.BlockSpec(memory_space=pl.ANY)], [prev]


def _ret_step(proj, cs, g, state, prev, layer):
    T = proj.shape[0]
    depth = state.shape[0]
    tb = min(T, STEP_TOKENS)
    wb = D_RET // LANES
    col = lambda j: (lambda i: (i, COL_RET // wb + j))
    st_shape, st_spec, extra_specs, extra_args = _stacked_state_specs(prev, depth, T, tb, H_RET, layer)
    n_in = 7
    return pl.pallas_call(
        _ret_step_kernel, grid=(T // tb,),
        in_specs=[pl.BlockSpec((tb, D_RET), col(0)), pl.BlockSpec((tb, D_RET), col(1)),
                  pl.BlockSpec((tb, D_RET), col(2)), pl.BlockSpec((tb, D_RET), col(3)),
                  pl.BlockSpec((2, tb, HEAD_DIM), lambda i: (0, i, 0)),
                  pl.BlockSpec((1, D_RET), lambda i: (0, 0)),
                  pl.BlockSpec((1, tb, H_RET, HEAD_DIM, HEAD_DIM), lambda i: (layer, i, 0, 0, 0))] + extra_specs,
        out_specs=[pl.BlockSpec((tb, D_RET), lambda i: (i, 0)), st_spec],
        out_shape=[jax.ShapeDtypeStruct((T, D_RET), BF16), st_shape],
        input_output_aliases={n_in: 1},
        compiler_params=_cparams("parallel"), name="ret_step")(proj, proj, proj, proj, cs, g, state, *extra_args)


def _mlstm_step_kernel(q_ref, k_ref, v_ref, og_ref, z_ref, gate_ref, bias_ref, g_ref, c_ref, n_ref, m_ref,
                       *rest):
    o_ref, co_ref, no_ref, mo_ref = rest[-4:]
    tb = q_ref.shape[0]
    gates = gate_ref[...] + bias_ref[...]
    m_all = m_ref[0]
    tok = lax.broadcasted_iota(jnp.int32, (tb, HEAD_DIM), 0)
    for h in range(H_ML):
        sl = slice(h * HEAD_DIM, (h + 1) * HEAD_DIM)
        q = q_ref[:, sl]
        k = k_ref[:, sl] * QK_SCALE
        v = v_ref[:, sl]
        ib = gates[:, GATE_I_LANE + h:GATE_I_LANE + h + 1]
        lf = _log_sigmoid(gates[:, GATE_F_LANE + h:GATE_F_LANE + h + 1])
        inter = lf + m_all[:, h:h + 1]
        m_t = jnp.maximum(inter, ib)
        w_intra = jnp.exp(ib - m_t)
        w_inter = jnp.exp(inter - m_t)
        nv = n_ref[0, :, h, :]
        qb = q.astype(BF16)
        kb = k.astype(BF16)
        vw = v * w_intra
        ro = jnp.zeros((tb, HEAD_DIM), F32)
        for t in range(tb):
            Cm = c_ref[0, t, h]
            ro = jnp.where(tok == t, _dot(qb, Cm.astype(BF16)), ro)
            vm = jnp.where(tok == t, vw, 0.0).astype(BF16)
            co_ref[0, t, h] = w_inter[t:t + 1, :] * Cm + _outer_t(kb, vm)
        s = jnp.sum(q * k, axis=1, keepdims=True) * w_intra
        num = s * v + w_inter * ro
        qn = s + w_inter * jnp.sum(q * nv, axis=1, keepdims=True)
        hb = num / (jnp.maximum(jnp.abs(qn), jnp.exp(-m_t)) + MLSTM_EPS)
        no_ref[:, h, :] = w_inter * nv + w_intra * k
        mo_ref[:, sl] = jnp.broadcast_to(m_t, (tb, HEAD_DIM))
        y = _head_norm(hb * jax.nn.sigmoid(og_ref[:, sl]), g_ref[:, sl]) * _silu(z_ref[:, sl])
        o_ref[:, sl] = y.astype(o_ref.dtype)


def _mlstm_step(proj, gate_bias, g, c_state, n_state, m_state, prev, layer):
    T = proj.shape[0]
    depth = c_state.shape[0]
    tb = min(T, STEP_TOKENS)
    wb = D_ML // LANES
    col = lambda j: (lambda i: (i, COL_ML // wb + j))
    st_shape, st_spec, extra_specs, extra_args = _stacked_state_specs(prev, depth, T, tb, H_ML, layer)
    n_in = 11
    return pl.pallas_call(
        _mlstm_step_kernel, grid=(T // tb,),
        in_specs=[pl.BlockSpec((tb, D_ML), col(0)), pl.BlockSpec((tb, D_ML), col(1)),
                  pl.BlockSpec((tb, D_ML), col(2)), pl.BlockSpec((tb, D_ML), col(3)),
                  pl.BlockSpec((tb, D_ML), col(4)),
                  pl.BlockSpec((tb, LANES), lambda i: (i, COL_GATE)),
                  pl.BlockSpec((1, LANES), lambda i: (0, 0)),
                  pl.BlockSpec((1, D_ML), lambda i: (0, 0)),
                  pl.BlockSpec((1, tb, H_ML, HEAD_DIM, HEAD_DIM), lambda i: (layer, i, 0, 0, 0)),
                  pl.BlockSpec((1, tb, H_ML, HEAD_DIM), lambda i: (layer, i, 0, 0)),
                  pl.BlockSpec((1, tb, H_ML), lambda i: (layer, i, 0))] + extra_specs,
        out_specs=[pl.BlockSpec((tb, D_ML), lambda i: (i, 0)), st_spec,
                   pl.BlockSpec((tb, H_ML, HEAD_DIM), lambda i: (i, 0, 0)),
                   pl.BlockSpec((tb, D_ML), lambda i: (i, 0))],
        out_shape=[jax.ShapeDtypeStruct((T, D_ML), BF16), st_shape,
                   jax.ShapeDtypeStruct((T, H_ML, HEAD_DIM), F32),
                   jax.ShapeDtypeStruct((T, D_ML), F32)],
        input_output_aliases={n_in: 1},
        compiler_params=_cparams("parallel"),
        name="mlstm_step")(proj, proj, proj, proj, proj, proj, gate_bias, g, c_state, n_state, m_state, *extra_args)


def _mla_common(cq_ref, ckv_ref, gate_ref, tab_ref, gq_ref, gkv_ref, wqn_ref, wqr_ref, ckvn_ref, kr_ref):
    tab = tab_ref[...]
    cqn = _rms(cq_ref[...], gq_ref[...]).astype(BF16)
    ckvn = _rms(ckv_ref[...], gkv_ref[...])
    ckvn_ref[...] = ckvn
    k_rope = _rope64(gate_ref[...], tab)
    kr_ref[...] = k_rope[:, :D_ROPE]
    q_nope = _dot_nt(cqn, wqn_ref[...])
    q_rope = _dot_nt(cqn, wqr_ref[...])
    return tab, ckvn, k_rope, q_nope, q_rope


def _mla_prep_prompt_kernel(cq_ref, ckv_ref, gate_ref, tab_ref, gq_ref, gkv_ref, wqn_ref, wqr_ref, wk_ref, wv_ref,
                            ckvn_ref, kr_ref, q_ref, k_ref, v_ref):
    tab, ckvn, k_rope, q_nope, q_rope = _mla_common(cq_ref, ckv_ref, gate_ref, tab_ref, gq_ref, gkv_ref,
                                                    wqn_ref, wqr_ref, ckvn_ref, kr_ref)
    cb = ckvn.astype(BF16)
    k_nope = _dot(cb, wk_ref[...])
    v_ref[...] = _dot(cb, wv_ref[...]).astype(v_ref.dtype)
    krb = k_rope.astype(k_ref.dtype)
    for h in range(H_MLA):
        sl = slice(h * HEAD_DIM, (h + 1) * HEAD_DIM)
        lo = slice(2 * h * LANES, (2 * h + 1) * LANES)
        hi = slice((2 * h + 1) * LANES, (2 * h + 2) * LANES)
        q_ref[:, lo] = (q_nope[:, sl] * ATT_SCALE).astype(q_ref.dtype)
        q_ref[:, hi] = (_rope64(q_rope[:, sl], tab) * ATT_SCALE).astype(q_ref.dtype)
        k_ref[:, lo] = k_nope[:, sl].astype(k_ref.dtype)
        k_ref[:, hi] = krb


def _mla_prep_sample_kernel(cq_ref, ckv_ref, gate_ref, tab_ref, gq_ref, gkv_ref, wqn_ref, wqr_ref, wukt_ref,
                            ckvn_ref, kr_ref, qs_ref):
    tab, ckvn, k_rope, q_nope, q_rope = _mla_common(cq_ref, ckv_ref, gate_ref, tab_ref, gq_ref, gkv_ref,
                                                    wqn_ref, wqr_ref, ckvn_ref, kr_ref)
    for h in range(H_MLA):
        sl = slice(h * HEAD_DIM, (h + 1) * HEAD_DIM)
        q_lat = _dot(q_nope[:, sl].astype(BF16), wukt_ref[h])
        qs_ref[h, :, 0:KV_RANK] = q_lat * ATT_SCALE
        qs_ref[h, :, KV_RANK:KV_RANK + LANES] = _rope64(q_rope[:, sl], tab) * ATT_SCALE


def _mla_prep(proj, tab, gq, gkv, wqn, wqr, extra_w, layer, sample):
    T = proj.shape[0]
    tm = min(T, 512)
    tab_blocks = tab.shape[1] // tm
    full = lambda a: pl.BlockSpec(a.shape, lambda i: (0,) * a.ndim)
    of_layer = lambda a: pl.BlockSpec((None,) + a.shape[1:], lambda i: (layer,) + (0,) * (a.ndim - 1))
    in_specs = [pl.BlockSpec((tm, Q_RANK), lambda i: (i, COL_CQ * LANES // Q_RANK)),
                pl.BlockSpec((tm, KV_RANK), lambda i: (i, COL_CKV * LANES // KV_RANK)),
                pl.BlockSpec((tm, LANES), lambda i: (i, COL_GATE)),
                pl.BlockSpec((3, tm, LANES), lambda i: (0, i % tab_blocks, 0)),
                full(gq), full(gkv), of_layer(wqn), of_layer(wqr)] + [of_layer(w) for w in extra_w]
    out_specs = [pl.BlockSpec((tm, KV_RANK), lambda i: (i, 0)), pl.BlockSpec((tm, D_ROPE), lambda i: (i, 0))]
    out_shape = [jax.ShapeDtypeStruct((T, KV_RANK), F32), jax.ShapeDtypeStruct((T, D_ROPE), F32)]
    if sample:
        kern = _mla_prep_sample_kernel
        out_specs.append(pl.BlockSpec((H_MLA, tm, KV_RANK + LANES), lambda i: (0, i, 0)))
        out_shape.append(jax.ShapeDtypeStruct((H_MLA, T, KV_RANK + LANES), F32))
    else:
        kern = _mla_prep_prompt_kernel
        out_specs += [pl.BlockSpec((tm, 2 * D_MLA), lambda i: (i, 0)), pl.BlockSpec((tm, 2 * D_MLA), lambda i: (i, 0)),
                      pl.BlockSpec((tm, D_MLA), lambda i: (i, 0))]
        out_shape += [jax.ShapeDtypeStruct((T, 2 * D_MLA), BF16), jax.ShapeDtypeStruct((T, 2 * D_MLA), BF16),
                      jax.ShapeDtypeStruct((T, D_MLA), BF16)]
    return pl.pallas_call(
        kern, grid=(T // tm,), in_specs=in_specs, out_specs=out_specs, out_shape=out_shape,
        compiler_params=_cparams("parallel"),
        name="mla_prep_sample" if sample else "mla_prep_prompt")(proj, proj, proj, tab, gq, gkv, wqn, wqr, *extra_w)


def _attn_prompt_kernel(q_ref, k_ref, v_ref, z_ref, o_ref, *, tq):
    L = q_ref.shape[0]
    row = lax.broadcasted_iota(jnp.int32, (tq, tq), 0)
    col = lax.broadcasted_iota(jnp.int32, (tq, tq), 1)
    for i in range(L // tq):
        cur = slice(i * tq, (i + 1) * tq)
        q = q_ref[cur, :]
        sd = jnp.where(col <= row, _dot_nt(q, k_ref[cur, :]), -jnp.inf)
        m = jnp.max(sd, axis=1, keepdims=True)
        if i > 0:
            so = _dot_nt(q, k_ref[0:i * tq, :])
            m = jnp.maximum(m, jnp.max(so, axis=1, keepdims=True))
            po = jnp.exp(so - m)
            l = jnp.sum(po, axis=1, keepdims=True)
            acc = _dot(po.astype(BF16), v_ref[0:i * tq, :])
        pd = jnp.exp(sd - m)
        if i > 0:
            l = l + jnp.sum(pd, axis=1, keepdims=True)
            acc = acc + _dot(pd.astype(BF16), v_ref[cur, :])
        else:
            l = jnp.sum(pd, axis=1, keepdims=True)
            acc = _dot(pd.astype(BF16), v_ref[cur, :])
        o_ref[cur, :] = (acc / l * _silu(z_ref[cur, :])).astype(o_ref.dtype)


def _attn_prompt(q, k, v, proj, B, L):
    tq = min(L, 512)
    return pl.pallas_call(
        functools.partial(_attn_prompt_kernel, tq=tq), grid=(B, H_MLA),
        in_specs=[pl.BlockSpec((L, 2 * LANES), lambda b, h: (b, h)),
                  pl.BlockSpec((L, 2 * LANES), lambda b, h: (b, h)),
                  pl.BlockSpec((L, HEAD_DIM), lambda b, h: (b, h)),
                  pl.BlockSpec((L, HEAD_DIM), lambda b, h: (b, COL_AZ + h))],
        out_specs=pl.BlockSpec((L, HEAD_DIM), lambda b, h: (b, h)),
        out_shape=jax.ShapeDtypeStruct((B * L, D_MLA), BF16),
        compiler_params=_cparams("parallel", "parallel"), name="attn_prompt")(q, k, v, proj)


def _decode_kernel(pt_ref, qs_ref, cn_ref, kn_ref, ckv_hbm, krt_hbm, o_ref, ckv0, ckv1, krt0, krt1, sem,
                   *, layer, n_pages, page):
    i = pl.program_id(0)
    n = pl.num_programs(0)
    bufs = ((ckv0, krt0), (ckv1, krt1))

    def page_copies(seq, s):
        out = []
        for p in range(n_pages):
            pg = pt_ref[seq, p]
            rows = pl.ds(p * page, page)
            out.append(pltpu.make_async_copy(ckv_hbm.at[layer, pg], bufs[s][0].at[rows, :], sem.at[0, s]))
            out.append(pltpu.make_async_copy(krt_hbm.at[layer, pg], bufs[s][1].at[:, rows], sem.at[1, s]))
        return out

    def start_fetch(seq, s):
        for cp in page_copies(seq, s):
            cp.start()

    def wait_fetch(seq, s):
        for cp in page_copies(seq, s):
            cp.wait()

    def attend(s):
        q = qs_ref[s]
        q16 = jnp.concatenate([q, jnp.zeros_like(q)], axis=0).astype(BF16)
        ql = q16[:, 0:KV_RANK]
        qr = q16[:, KV_RANK:KV_RANK + D_ROPE]
        kc = bufs[s][0][...].astype(BF16)
        sc = _dot_nt(ql, kc) + _dot(qr, bufs[s][1][...].astype(BF16))
        cn = cn_ref[s].astype(BF16).astype(F32)
        kn = kn_ref[s].astype(BF16).astype(F32)
        s_new = (jnp.sum(ql.astype(F32) * cn, axis=1, keepdims=True)
                 + jnp.sum(qr.astype(F32) * kn, axis=1, keepdims=True))
        m = jnp.maximum(jnp.max(sc, axis=1, keepdims=True), s_new)
        p = jnp.exp(sc - m)
        p_new = jnp.exp(s_new - m)
        l = jnp.sum(p, axis=1, keepdims=True) + p_new
        pb = p.astype(BF16)
        half = kc.shape[0] // 2
        acc = (_dot(pb[:, :half], kc[:half]) + _dot(pb[:, half:], kc[half:])
               + p_new.astype(BF16).astype(F32) * cn)
        o_ref[s] = (acc / l)[0:H_MLA, :]

    @pl.when(i == 0)
    def _():
        start_fetch(0, 0)

    wait_fetch(2 * i, 0)
    start_fetch(2 * i + 1, 1)
    attend(0)
    wait_fetch(2 * i + 1, 1)

    @pl.when(i + 1 < n)
    def _():
        start_fetch(2 * i + 2, 0)

    attend(1)


def _decode(page_table, qs, ckv_new, kr_new, cache_ckv, cache_krope_t, layer):
    DB, n_pages = page_table.shape
    page = cache_ckv.shape[2]
    P = n_pages * page
    assert DB % 2 == 0
    kern = functools.partial(_decode_kernel, layer=layer, n_pages=n_pages, page=page)
    grid_spec = pltpu.PrefetchScalarGridSpec(
        num_scalar_prefetch=1, grid=(DB // 2,),
        in_specs=[pl.BlockSpec((2, H_MLA, KV_RANK + LANES), lambda b, pt: (b, 0, 0)),
                  pl.BlockSpec((2, 1, KV_RANK), lambda b, pt: (b, 0, 0)),
                  pl.BlockSpec((2, 1, D_ROPE), lambda b, pt: (b, 0, 0)),
                  pl.BlockSpec(memory_space=pl.ANY), pl.BlockSpec(memory_space=pl.ANY)],
        out_specs=pl.BlockSpec((2, H_MLA, KV_RANK), lambda b, pt: (b, 0, 0)),
        scratch_shapes=[pltpu.VMEM((P, KV_RANK), F32), pltpu.VMEM((P, KV_RANK), F32),
                        pltpu.VMEM((D_ROPE, P), F32), pltpu.VMEM((D_ROPE, P), F32),
                        pltpu.SemaphoreType.DMA((2, 2))])
    return pl.pallas_call(
        kern, grid_spec=grid_spec, out_shape=jax.ShapeDtypeStruct((DB, H_MLA, KV_RANK), F32),
        compiler_params=_cparams("arbitrary"), name="mla_decode")(
            page_table, qs, ckv_new.reshape(DB, 1, KV_RANK), kr_new.reshape(DB, 1, D_ROPE), cache_ckv, cache_krope_t)


def _mla_out_kernel(ol_ref, w_ref, z_ref, o_ref):
    o = _dot(ol_ref[0].astype(BF16), w_ref[...])
    o_ref[...] = (o * _silu(z_ref[...])).astype(o_ref.dtype)


def _mla_out(o_lat_h, wuv, layer, proj):
    T = proj.shape[0]
    return pl.pallas_call(
        _mla_out_kernel, grid=(H_MLA,),
        in_specs=[pl.BlockSpec((1, T, KV_RANK), lambda h: (h, 0, 0)),
                  pl.BlockSpec((None, None, KV_RANK, HEAD_DIM), lambda h: (layer, h, 0, 0)),
                  pl.BlockSpec((T, HEAD_DIM), lambda h: (0, COL_AZ + h))],
        out_specs=pl.BlockSpec((T, HEAD_DIM), lambda h: (0, h)),
        out_shape=jax.ShapeDtypeStruct((T, D_MLA), BF16),
        compiler_params=_cparams("parallel"), name="mla_out")(o_lat_h, wuv, proj)


def _out_proj_kernel(or_ref, om_ref, oa_ref, x_ref, w_ref, g_ref, *out_refs, last):
    x = (x_ref[...] + _dot(or_ref[...], w_ref[0:D_RET, :]) + _dot(om_ref[...], w_ref[D_RET:D_RET + D_ML, :])
         + _dot(oa_ref[...], w_ref[D_RET + D_ML:, :]))
    hn = _rms(x, g_ref[...])
    if last:
        out_refs[0][...] = hn
    else:
        out_refs[0][...] = x
        out_refs[1][...] = hn.astype(out_refs[1].dtype)


def _out_proj(o_ret, o_ml, o_mla, x, w, layer, g, last):
    T, D = x.shape
    tm = min(T, 256)
    row = lambda n: pl.BlockSpec((tm, n), lambda i: (i, 0))
    if last:
        out_specs, out_shape = [row(D)], [jax.ShapeDtypeStruct((T, D), F32)]
    else:
        out_specs = [row(D), row(D)]
        out_shape = [jax.ShapeDtypeStruct((T, D), F32), jax.ShapeDtypeStruct((T, D), BF16)]
    return pl.pallas_call(
        functools.partial(_out_proj_kernel, last=last), grid=(T // tm,),
        in_specs=[row(D_RET), row(D_ML), row(D_MLA), row(D),
                  pl.BlockSpec((None,) + w.shape[1:], lambda i: (layer, 0, 0)),
                  pl.BlockSpec((1, D), lambda i: (0, 0))],
        out_specs=out_specs, out_shape=out_shape,
        compiler_params=_cparams("parallel"), name="out_proj")(o_ret, o_ml, o_mla, x, w, g.reshape(1, D))


def _rope_tables(pos):
    pos = pos.astype(F32)[:, None]

    def cs(half):
        inv = ROPE_BASE ** (-jnp.arange(half, dtype=F32) / half)
        ang = pos * inv[None, :]
        return jnp.cos(ang), jnp.sin(ang)

    c64, s64 = cs(HEAD_DIM // 2)
    tab128 = jnp.stack([jnp.concatenate([c64, c64], -1), jnp.concatenate([-s64, s64], -1)])
    c32, s32 = cs(D_ROPE // 2)
    z32 = jnp.zeros_like(c32)
    tab64 = jnp.stack([jnp.concatenate([c32, c32, z32, z32], -1),
                       jnp.concatenate([-s32, z32, z32, z32], -1),
                       jnp.concatenate([z32, s32, z32, z32], -1)])
    return tab128, tab64


def _layout_w_in(w_in):
    wt = jnp.swapaxes(w_in, 1, 2)
    o = np.cumsum((0,) + IN_WIDTHS)
    seg = lambda a, b: wt[:, o[a]:o[b], :]
    pad = jnp.zeros((wt.shape[0], N_IN_PAD - D_IN, wt.shape[2]), wt.dtype)
    return jnp.concatenate([seg(0, 9), seg(11, 12), seg(12, 13), seg(14, 15), seg(13, 14), seg(9, 11), pad],
                           axis=1).astype(BF16)


def kernel(x_prompt, x_sample, state_ret, state_mlstm_c, state_mlstm_n, state_mlstm_m, cache_ckv, cache_krope,
           page_table, g_norm, w_in, b_ig, b_fg, g_ret, g_ml, g_q, g_kv, w_uq, w_uk, w_uv, w_out, g_final):
    B, L, D = x_prompt.shape
    DB, Ls, _ = x_sample.shape
    assert Ls == 1, "the sample group is one new token per sequence"
    depth = w_in.shape[0]
    past_len = page_table.shape[1] * cache_ckv.shape[2]

    tab128_p, tab64_p = _rope_tables(jnp.arange(L))
    tab128_s, tab64_s = _rope_tables(past_len + jnp.arange(Ls))
    tab128_s = jnp.broadcast_to(tab128_s, (2, DB, HEAD_DIM))
    tab64_s = jnp.broadcast_to(tab64_s, (3, DB, LANES))
    w_in_b = _layout_w_in(w_in)
    w_out_b = w_out.astype(BF16)
    w_uq_t = jnp.transpose(w_uq, (0, 2, 3, 1))
    wqn = w_uq_t[:, :, :D_NOPE, :].reshape(depth, D_MLA, Q_RANK).astype(BF16)
    wqr = jnp.concatenate([w_uq_t[:, :, D_NOPE:, :], jnp.zeros((depth, H_MLA, LANES - D_ROPE, Q_RANK), w_uq.dtype)],
                          2).reshape(depth, D_MLA, Q_RANK).astype(BF16)
    wk = w_uk.reshape(depth, KV_RANK, D_MLA).astype(BF16)
    wv = w_uv.reshape(depth, KV_RANK, D_MLA).astype(BF16)
    wukt = jnp.transpose(w_uk, (0, 2, 3, 1)).astype(BF16)
    wuv_h = jnp.transpose(w_uv, (0, 2, 1, 3)).astype(BF16)
    zeros_gate = jnp.zeros((depth, LANES - D_ROPE - 2 * H_ML), F32)
    gate_bias = jnp.concatenate([jnp.zeros((depth, D_ROPE), F32), b_ig, b_fg, zeros_gate], -1)

    cache_krope_t = jnp.swapaxes(cache_krope, 2, 3)

    xp = x_prompt.reshape(B * L, D)
    xs = x_sample.reshape(DB, D)
    hp = _rmsnorm(xp, g_norm[0], BF16)
    hs = _rmsnorm(xs, g_norm[0], BF16)
    p_out = [[] for _ in range(6)]
    s_out = [[] for _ in range(4)]
    s_ret_all = s_c_all = None
    for l in range(depth):
        last = l == depth - 1
        g_next = g_final if last else g_norm[l + 1]
        gq, gkv = g_q[l].reshape(1, Q_RANK), g_kv[l].reshape(1, KV_RANK)
        gr, gm = g_ret[l].reshape(1, D_RET), g_ml[l].reshape(1, D_ML)
        gb = gate_bias[l].reshape(1, LANES)

        proj = _in_proj(hp, w_in_b, l)
        o_ret, st_ret = _ret_prompt(proj, tab128_p, gr, B, L)
        o_ml, st_c, st_n, st_m = _mlstm_prompt(proj, gb, gm, B, L)
        ckvn, krope, q, k, v = _mla_prep(proj, tab64_p, gq, gkv, wqn, wqr, (wk, wv), l, sample=False)
        o_mla = _attn_prompt(q, k, v, proj, B, L)
        res = _out_proj(o_ret, o_ml, o_mla, xp, w_out_b, l, g_next, last)
        if last:
            y_prompt = res[0]
        else:
            xp, hp = res
        for j, t in enumerate((st_ret, st_c, st_n[:, :, 0, :], st_m[:, :, 0, 0],
                               ckvn.reshape(B, L, KV_RANK), krope.reshape(B, L, D_ROPE))):
            p_out[j].append(t)

        proj_s = _in_proj(hs, w_in_b, l)
        o_ret, s_ret_all = _ret_step(proj_s, tab128_s, gr, state_ret, s_ret_all, l)
        o_ml, s_c_all, st_n, st_m = _mlstm_step(proj_s, gb, gm, state_mlstm_c, state_mlstm_n, state_mlstm_m,
                                                s_c_all, l)
        ckvn, krope, qs = _mla_prep(proj_s, tab64_s, gq, gkv, wqn, wqr, (wukt,), l, sample=True)
        o_lat = _decode(page_table, jnp.transpose(qs, (1, 0, 2)), ckvn, krope, cache_ckv, cache_krope_t, l)
        o_mla = _mla_out(jnp.transpose(o_lat, (1, 0, 2)), wuv_h, l, proj_s)
        res = _out_proj(o_ret, o_ml, o_mla, xs, w_out_b, l, g_next, last)
        if last:
            y_sample = res[0]
        else:
            xs, hs = res
        for j, t in enumerate((st_n, st_m.reshape(DB, H_ML, HEAD_DIM)[:, :, 0],
                               ckvn.reshape(DB, Ls, KV_RANK), krope.reshape(DB, Ls, D_ROPE))):
            s_out[j].append(t)

    return (y_prompt.reshape(B, L, D), y_sample.reshape(DB, Ls, D),
            *[jnp.stack(t) for t in p_out], s_ret_all, s_c_all, *[jnp.stack(t) for t in s_out])
```

```python
import functools

import numpy as np
import jax
import jax.numpy as jnp
from jax import lax
from jax.experimental import pallas as pl
from jax.experimental.pallas import tpu as pltpu

F32 = jnp.float32
BF16 = jnp.bfloat16

HEAD_DIM = 128
H_RET = 4
H_ML = 4
H_MLA = 8
D_RET = H_RET * HEAD_DIM
D_ML = H_ML * HEAD_DIM
D_MLA = H_MLA * HEAD_DIM
Q_RANK = 512
KV_RANK = 256
D_NOPE = 128
D_ROPE = 64
ROPE_BASE = 10000.0
MIX_CHUNK = 128
NORM_EPS = 1e-6
MLSTM_EPS = 1e-6
ATT_SCALE = (D_NOPE + D_ROPE) ** -0.5
QK_SCALE = HEAD_DIM ** -0.5
IN_WIDTHS = (D_RET, D_RET, D_RET, D_RET, D_ML, D_ML, D_ML, D_ML, D_ML, H_ML, H_ML,
             Q_RANK, KV_RANK, D_ROPE, D_MLA)
D_IN = sum(IN_WIDTHS)

LANES = 128
VMEM_LIMIT_BYTES = 56 * 1024 * 1024

COL_RET = 0
COL_ML = 16
COL_CQ = 36
COL_CKV = 40
COL_AZ = 42
COL_GATE = 50
GATE_I_LANE = D_ROPE
GATE_F_LANE = D_ROPE + H_ML
N_IN_PAD = 52 * LANES
RET_LOG_G = [float(np.log1p(-np.exp2(np.float32(-5.0 - h)))) for h in range(H_RET)]


def _cparams(*sem):
    return pltpu.CompilerParams(dimension_semantics=sem, vmem_limit_bytes=VMEM_LIMIT_BYTES)


def _dot(a, b):
    return jnp.dot(a, b, preferred_element_type=F32)


def _dot_nt(a, b):
    return lax.dot_general(a, b, (((1,), (1,)), ((), ())), preferred_element_type=F32)


def _silu(x):
    return x * jax.nn.sigmoid(x)


def _rms(x, g):
    return x * lax.rsqrt(jnp.mean(x * x, axis=-1, keepdims=True) + NORM_EPS) * g


def _head_norm(o, g):
    mu = jnp.mean(o, axis=-1, keepdims=True)
    d = o - mu
    var = jnp.mean(d * d, axis=-1, keepdims=True)
    return d * lax.rsqrt(var + NORM_EPS) * g


def _rope128(x, cos, sin_signed):
    return x * cos + pltpu.roll(x, HEAD_DIM // 2, axis=1) * sin_signed


def _rope128b(x, cos, sin_signed):
    return x * cos + pltpu.roll(x, HEAD_DIM // 2, axis=2) * sin_signed


def _rope64(x, tab):
    q = D_ROPE // 2
    return x * tab[0] + pltpu.roll(x, LANES - q, axis=1) * tab[1] + pltpu.roll(x, q, axis=1) * tab[2]


def _rmsnorm_kernel(x_ref, g_ref, o_ref):
    o_ref[...] = _rms(x_ref[...], g_ref[...]).astype(o_ref.dtype)


def _rmsnorm(x, g, dtype):
    T, D = x.shape
    tm = min(T, 512)
    return pl.pallas_call(
        _rmsnorm_kernel, grid=(T // tm,),
        in_specs=[pl.BlockSpec((tm, D), lambda i: (i, 0)), pl.BlockSpec((1, D), lambda i: (0, 0))],
        out_specs=pl.BlockSpec((tm, D), lambda i: (i, 0)),
        out_shape=jax.ShapeDtypeStruct((T, D), dtype),
        compiler_params=_cparams("parallel"), name="rmsnorm")(x, g.reshape(1, D))


def _matmul_nt_kernel(h_ref, wt_ref, o_ref):
    o_ref[...] = _dot_nt(h_ref[...], wt_ref[...])


def _in_proj(hn, wt, layer):
    T, D = hn.shape
    N = wt.shape[1]
    tm = min(T, 2048)
    tn = 4 * LANES
    return pl.pallas_call(
        _matmul_nt_kernel, grid=(T // tm, N // tn),
        in_specs=[pl.BlockSpec((tm, D), lambda i, j: (i, 0)),
                  pl.BlockSpec((None, tn, D), lambda i, j: (layer, j, 0))],
        out_specs=pl.BlockSpec((tm, tn), lambda i, j: (i, j)),
        out_shape=jax.ShapeDtypeStruct((T, N), F32),
        compiler_params=_cparams("parallel", "arbitrary"), name="in_proj")(hn, wt)


def _ret_prompt_kernel(q_ref, k_ref, v_ref, z_ref, cs_ref, g_ref, o_ref, s_ref, S_scr):
    ci = pl.program_id(0)
    B, c = q_ref.shape[0], q_ref.shape[1]

    @pl.when(ci == 0)
    def _():
        S_scr[...] = jnp.zeros_like(S_scr)

    cos = cs_ref[0]
    sin = cs_ref[1]
    row = lax.broadcasted_iota(jnp.int32, (c, c), 0)
    col = lax.broadcasted_iota(jnp.int32, (c, c), 1)
    rel = (row - col).astype(F32)
    idx = lax.broadcasted_iota(jnp.int32, (c, 1), 0).astype(F32)
    for h in range(H_RET):
        sl = slice(h * HEAD_DIM, (h + 1) * HEAD_DIM)
        lg = RET_LOG_G[h]
        decay = jnp.where(rel >= 0, jnp.exp(lg * jnp.maximum(rel, 0.0)), 0.0)
        q_dec = jnp.exp(lg * (idx + 1.0))
        k_dec = jnp.exp(lg * (c - 1.0 - idx))
        bmm = lambda e, x, y: jnp.einsum(e, x, y, preferred_element_type=F32)
        q = _rope128b(q_ref[:, :, sl], cos, sin)
        k = _rope128b(k_ref[:, :, sl], cos, sin) * QK_SCALE
        vb = v_ref[:, :, sl].astype(BF16)
        qb = q.astype(BF16)
        S = S_scr[:, h]
        sc = bmm('bik,bjk->bij', qb, k.astype(BF16)) * decay
        o = bmm('bij,bjv->biv', sc.astype(BF16), vb) + bmm('bik,bkv->biv', qb, S.astype(BF16)) * q_dec
        kdt = jnp.swapaxes(k * k_dec, 1, 2).astype(BF16)
        S_scr[:, h] = S * float(np.exp(lg * c)) + bmm('bkj,bjv->bkv', kdt, vb)
        y = _head_norm(o, g_ref[:, sl]) * _silu(z_ref[:, :, sl])
        o_ref[:, :, sl] = y.astype(o_ref.dtype)

    @pl.when(ci == pl.num_programs(0) - 1)
    def _():
        s_ref[...] = S_scr[...]


def _ret_prompt(proj, cs, g, B, L):
    c = MIX_CHUNK if L % MIX_CHUNK == 0 else L
    wb = D_RET // LANES
    col = lambda j: (lambda i: (0, i, COL_RET // wb + j))
    proj3 = proj.reshape(B, L, proj.shape[1])
    o, st = pl.pallas_call(
        _ret_prompt_kernel, grid=(L // c,),
        in_specs=[pl.BlockSpec((B, c, D_RET), col(0)), pl.BlockSpec((B, c, D_RET), col(1)),
                  pl.BlockSpec((B, c, D_RET), col(2)), pl.BlockSpec((B, c, D_RET), col(3)),
                  pl.BlockSpec((2, c, HEAD_DIM), lambda i: (0, i, 0)),
                  pl.BlockSpec((1, D_RET), lambda i: (0, 0))],
        out_specs=[pl.BlockSpec((B, c, D_RET), lambda i: (0, i, 0)),
                   pl.BlockSpec((B, H_RET, HEAD_DIM, HEAD_DIM), lambda i: (0, 0, 0, 0))],
        out_shape=[jax.ShapeDtypeStruct((B, L, D_RET), BF16),
                   jax.ShapeDtypeStruct((B, H_RET, HEAD_DIM, HEAD_DIM), F32)],
        scratch_shapes=[pltpu.VMEM((B, H_RET, HEAD_DIM, HEAD_DIM), F32)],
        compiler_params=_cparams("arbitrary"), name="ret_prompt")(proj3, proj3, proj3, proj3, cs, g)
    return o.reshape(B * L, D_RET), st


def _log_sigmoid(x):
    return -(jnp.maximum(-x, 0.0) + jnp.log1p(jnp.exp(-jnp.abs(x))))


def _mlstm_prompt_kernel(q_ref, k_ref, v_ref, og_ref, z_ref, gate_ref, bias_ref, g_ref,
                         o_ref, c_ref, n_ref, m_ref, C_scr, M_scr):
    ci = pl.program_id(0)
    B, c = q_ref.shape[0], q_ref.shape[1]

    @pl.when(ci == 0)
    def _():
        C_scr[...] = jnp.zeros_like(C_scr)
        M_scr[...] = jnp.zeros_like(M_scr)

    row = lax.broadcasted_iota(jnp.int32, (c, c), 0)
    col = lax.broadcasted_iota(jnp.int32, (c, c), 1)
    causal = col <= row
    eye = col == row
    ones = jnp.ones((B, c, HEAD_DIM), BF16)
    gates = gate_ref[...] + bias_ref[...]
    bmm = lambda e, x, y: jnp.einsum(e, x, y, preferred_element_type=F32)
    for h in range(H_ML):
        sl = slice(h * HEAD_DIM, (h + 1) * HEAD_DIM)
        q = q_ref[:, :, sl]
        k = k_ref[:, :, sl] * QK_SCALE
        vb1 = jnp.concatenate([v_ref[:, :, sl].astype(BF16), ones], axis=2)
        i_col = gates[:, :, GATE_I_LANE + h:GATE_I_LANE + h + 1]
        lf_col = _log_sigmoid(gates[:, :, GATE_F_LANE + h:GATE_F_LANE + h + 1])
        lf_row = jnp.sum(jnp.where(eye, lf_col, 0.0), axis=1, keepdims=True)
        i_row = jnp.sum(jnp.where(eye, i_col, 0.0), axis=1, keepdims=True)
        b_col = jnp.sum(jnp.where(causal, lf_row, 0.0), axis=2, keepdims=True)
        b_row = jnp.sum(jnp.where(row <= col, lf_col, 0.0), axis=1, keepdims=True)
        m_prev = M_scr[:, h][:, :, 0:1]
        log_d = jnp.where(causal, b_col - b_row + i_row, -jnp.inf)
        inter = b_col + m_prev
        m_t = jnp.maximum(inter, jnp.max(log_d, axis=2, keepdims=True))
        w_intra = jnp.exp(log_d - m_t)
        w_inter = jnp.exp(inter - m_t)
        qb = q.astype(BF16)
        C1 = C_scr[:, h]
        s = bmm('bik,bjk->bij', qb, k.astype(BF16)) * w_intra
        a = bmm('bij,bjv->biv', s.astype(BF16), vb1) + w_inter * bmm('bik,bkv->biv', qb, C1.astype(BF16))
        hb = a[:, :, :HEAD_DIM] / (jnp.maximum(jnp.abs(a[:, :, HEAD_DIM:]), jnp.exp(-m_t)) + MLSTM_EPS)
        m_new = m_t[:, c - 1:c, :]
        b_last = b_col[:, c - 1:c, :]
        w_k = jnp.exp(b_last - b_col + i_col - m_new)
        dec = jnp.exp(b_last + m_prev - m_new)
        kwt = jnp.swapaxes(k * w_k, 1, 2).astype(BF16)
        C_scr[:, h] = dec * C1 + bmm('bkj,bjv->bkv', kwt, vb1)
        M_scr[:, h] = jnp.broadcast_to(m_new, (B, 1, HEAD_DIM))
        y = _head_norm(hb * jax.nn.sigmoid(og_ref[:, :, sl]), g_ref[:, sl]) * _silu(z_ref[:, :, sl])
        o_ref[:, :, sl] = y.astype(o_ref.dtype)

    @pl.when(ci == pl.num_programs(0) - 1)
    def _():
        m_ref[...] = M_scr[...]
        for b in range(B):
            for h in range(H_ML):
                C1 = C_scr[b, h]
                c_ref[b, h] = C1[:, :HEAD_DIM]
                n_ref[b, h] = C1[:, HEAD_DIM:].T[0:1, :]


def _mlstm_prompt(proj, gate_bias, g, B, L):
    c = MIX_CHUNK if L % MIX_CHUNK == 0 else L
    wb = D_ML // LANES
    col = lambda j: (lambda i: (0, i, COL_ML // wb + j))
    proj3 = proj.reshape(B, L, proj.shape[1])
    vec = jax.ShapeDtypeStruct((B, H_ML, 1, HEAD_DIM), F32)
    vec_spec = pl.BlockSpec((B, H_ML, 1, HEAD_DIM), lambda i: (0, 0, 0, 0))
    o, st_c, st_n, st_m = pl.pallas_call(
        _mlstm_prompt_kernel, grid=(L // c,),
        in_specs=[pl.BlockSpec((B, c, D_ML), col(0)), pl.BlockSpec((B, c, D_ML), col(1)),
                  pl.BlockSpec((B, c, D_ML), col(2)), pl.BlockSpec((B, c, D_ML), col(3)),
                  pl.BlockSpec((B, c, D_ML), col(4)),
                  pl.BlockSpec((B, c, LANES), lambda i: (0, i, COL_GATE)),
                  pl.BlockSpec((1, LANES), lambda i: (0, 0)),
                  pl.BlockSpec((1, D_ML), lambda i: (0, 0))],
        out_specs=[pl.BlockSpec((B, c, D_ML), lambda i: (0, i, 0)),
                   pl.BlockSpec((B, H_ML, HEAD_DIM, HEAD_DIM), lambda i: (0, 0, 0, 0)),
                   vec_spec, vec_spec],
        out_shape=[jax.ShapeDtypeStruct((B, L, D_ML), BF16),
                   jax.ShapeDtypeStruct((B, H_ML, HEAD_DIM, HEAD_DIM), F32), vec, vec],
        scratch_shapes=[pltpu.VMEM((B, H_ML, HEAD_DIM, 2 * HEAD_DIM), F32),
                        pltpu.VMEM((B, H_ML, 1, HEAD_DIM), F32)],
        compiler_params=_cparams("arbitrary"),
        name="mlstm_prompt")(proj3, proj3, proj3, proj3, proj3, proj3, gate_bias, g)
    return o.reshape(B * L, D_ML), st_c, st_n, st_m


STEP_TOKENS = 16


def _outer_t(kb, vm):
    return lax.dot_general(kb, vm, (((0,), (0,)), ((), ())), preferred_element_type=F32)


def _ret_step_kernel(q_ref, k_ref, v_ref, z_ref, cs_ref, g_ref, s_ref, *rest):
    o_ref, so_ref = rest[-2:]
    tb = q_ref.shape[0]
    cos = cs_ref[0]
    sin = cs_ref[1]
    tok = lax.broadcasted_iota(jnp.int32, (tb, HEAD_DIM), 0)
    for h in range(H_RET):
        sl = slice(h * HEAD_DIM, (h + 1) * HEAD_DIM)
        gamma = float(np.exp(RET_LOG_G[h]))
        q = _rope128(q_ref[:, sl], cos, sin)
        k = _rope128(k_ref[:, sl], cos, sin) * QK_SCALE
        v = v_ref[:, sl]
        qb = q.astype(BF16)
        kb = k.astype(BF16)
        ro = jnp.zeros((tb, HEAD_DIM), F32)
        for t in range(tb):
            S = s_ref[0, t, h]
            ro = jnp.where(tok == t, _dot(qb, S.astype(BF16)), ro)
            vm = jnp.where(tok == t, v, 0.0).astype(BF16)
            so_ref[0, t, h] = gamma * S + _outer_t(kb, vm)
        o = jnp.sum(q * k, axis=1, keepdims=True) * v + gamma * ro
        y = _head_norm(o, g_ref[:, sl]) * _silu(z_ref[:, sl])
        o_ref[:, sl] = y.astype(o_ref.dtype)


def _stacked_state_specs(prev, depth, T, tb, H, layer):
    shape = jax.ShapeDtypeStruct((depth, T, H, HEAD_DIM, HEAD_DIM), F32)
    spec = pl.BlockSpec((1, tb, H, HEAD_DIM, HEAD_DIM), lambda i: (layer, i, 0, 0, 0))
    if prev is None:
        prev = jnp.zeros(shape.shape, shape.dtype)
    return shape, spec, [pl.BlockSpec(memory_space=pl.ANY)], [prev]


def _ret_step(proj, cs, g, state, prev, layer):
    T = proj.shape[0]
    depth = state.shape[0]
    tb = min(T, STEP_TOKENS)
    wb = D_RET // LANES
    col = lambda j: (lambda i: (i, COL_RET // wb + j))
    st_shape, st_spec, extra_specs, extra_args = _stacked_state_specs(prev, depth, T, tb, H_RET, layer)
    n_in = 7
    return pl.pallas_call(
        _ret_step_kernel, grid=(T // tb,),
        in_specs=[pl.BlockSpec((tb, D_RET), col(0)), pl.BlockSpec((tb, D_RET), col(1)),
                  pl.BlockSpec((tb, D_RET), col(2)), pl.BlockSpec((tb, D_RET), col(3)),
                  pl.BlockSpec((2, tb, HEAD_DIM), lambda i: (0, i, 0)),
                  pl.BlockSpec((1, D_RET), lambda i: (0, 0)),
                  pl.BlockSpec((1, tb, H_RET, HEAD_DIM, HEAD_DIM), lambda i: (layer, i, 0, 0, 0))] + extra_specs,
        out_specs=[pl.BlockSpec((tb, D_RET), lambda i: (i, 0)), st_spec],
        out_shape=[jax.ShapeDtypeStruct((T, D_RET), BF16), st_shape],
        input_output_aliases={n_in: 1},
        compiler_params=_cparams("parallel"), name="ret_step")(proj, proj, proj, proj, cs, g, state, *extra_args)


def _mlstm_step_kernel(q_ref, k_ref, v_ref, og_ref, z_ref, gate_ref, bias_ref, g_ref, c_ref, n_ref, m_ref,
                       *rest):
    o_ref, co_ref, no_ref, mo_ref = rest[-4:]
    tb = q_ref.shape[0]
    gates = gate_ref[...] + bias_ref[...]
    m_all = m_ref[0]
    tok = lax.broadcasted_iota(jnp.int32, (tb, HEAD_DIM), 0)
    for h in range(H_ML):
        sl = slice(h * HEAD_DIM, (h + 1) * HEAD_DIM)
        q = q_ref[:, sl]
        k = k_ref[:, sl] * QK_SCALE
        v = v_ref[:, sl]
        ib = gates[:, GATE_I_LANE + h:GATE_I_LANE + h + 1]
        lf = _log_sigmoid(gates[:, GATE_F_LANE + h:GATE_F_LANE + h + 1])
        inter = lf + m_all[:, h:h + 1]
        m_t = jnp.maximum(inter, ib)
        w_intra = jnp.exp(ib - m_t)
        w_inter = jnp.exp(inter - m_t)
        nv = n_ref[0, :, h, :]
        qb = q.astype(BF16)
        kb = k.astype(BF16)
        vw = v * w_intra
        ro = jnp.zeros((tb, HEAD_DIM), F32)
        for t in range(tb):
            Cm = c_ref[0, t, h]
            ro = jnp.where(tok == t, _dot(qb, Cm.astype(BF16)), ro)
            vm = jnp.where(tok == t, vw, 0.0).astype(BF16)
            co_ref[0, t, h] = w_inter[t:t + 1, :] * Cm + _outer_t(kb, vm)
        s = jnp.sum(q * k, axis=1, keepdims=True) * w_intra
        num = s * v + w_inter * ro
        qn = s + w_inter * jnp.sum(q * nv, axis=1, keepdims=True)
        hb = num / (jnp.maximum(jnp.abs(qn), jnp.exp(-m_t)) + MLSTM_EPS)
        no_ref[:, h, :] = w_inter * nv + w_intra * k
        mo_ref[:, sl] = jnp.broadcast_to(m_t, (tb, HEAD_DIM))
        y = _head_norm(hb * jax.nn.sigmoid(og_ref[:, sl]), g_ref[:, sl]) * _silu(z_ref[:, sl])
        o_ref[:, sl] = y.astype(o_ref.dtype)


def _mlstm_step(proj, gate_bias, g, c_state, n_state, m_state, prev, layer):
    T = proj.shape[0]
    depth = c_state.shape[0]
    tb = min(T, STEP_TOKENS)
    wb = D_ML // LANES
    col = lambda j: (lambda i: (i, COL_ML // wb + j))
    st_shape, st_spec, extra_specs, extra_args = _stacked_state_specs(prev, depth, T, tb, H_ML, layer)
    n_in = 11
    return pl.pallas_call(
        _mlstm_step_kernel, grid=(T // tb,),
        in_specs=[pl.BlockSpec((tb, D_ML), col(0)), pl.BlockSpec((tb, D_ML), col(1)),
                  pl.BlockSpec((tb, D_ML), col(2)), pl.BlockSpec((tb, D_ML), col(3)),
                  pl.BlockSpec((tb, D_ML), col(4)),
                  pl.BlockSpec((tb, LANES), lambda i: (i, COL_GATE)),
                  pl.BlockSpec((1, LANES), lambda i: (0, 0)),
                  pl.BlockSpec((1, D_ML), lambda i: (0, 0)),
                  pl.BlockSpec((1, tb, H_ML, HEAD_DIM, HEAD_DIM), lambda i: (layer, i, 0, 0, 0)),
                  pl.BlockSpec((1, tb, H_ML, HEAD_DIM), lambda i: (layer, i, 0, 0)),
                  pl.BlockSpec((1, tb, H_ML), lambda i: (layer, i, 0))] + extra_specs,
        out_specs=[pl.BlockSpec((tb, D_ML), lambda i: (i, 0)), st_spec,
                   pl.BlockSpec((tb, H_ML, HEAD_DIM), lambda i: (i, 0, 0)),
                   pl.BlockSpec((tb, D_ML), lambda i: (i, 0))],
        out_shape=[jax.ShapeDtypeStruct((T, D_ML), BF16), st_shape,
                   jax.ShapeDtypeStruct((T, H_ML, HEAD_DIM), F32),
                   jax.ShapeDtypeStruct((T, D_ML), F32)],
        input_output_aliases={n_in: 1},
        compiler_params=_cparams("parallel"),
        name="mlstm_step")(proj, proj, proj, proj, proj, proj, gate_bias, g, c_state, n_state, m_state, *extra_args)


def _mla_common(cq_ref, ckv_ref, gate_ref, tab_ref, gq_ref, gkv_ref, wqn_ref, wqr_ref, ckvn_ref, kr_ref):
    tab = tab_ref[...]
    cqn = _rms(cq_ref[...], gq_ref[...]).astype(BF16)
    ckvn = _rms(ckv_ref[...], gkv_ref[...])
    ckvn_ref[...] = ckvn
    k_rope = _rope64(gate_ref[...], tab)
    kr_ref[...] = k_rope[:, :D_ROPE]
    q_nope = _dot_nt(cqn, wqn_ref[...])
    q_rope = _dot_nt(cqn, wqr_ref[...])
    return tab, ckvn, k_rope, q_nope, q_rope


def _mla_prep_prompt_kernel(cq_ref, ckv_ref, gate_ref, tab_ref, gq_ref, gkv_ref, wqn_ref, wqr_ref, wk_ref, wv_ref,
                            ckvn_ref, kr_ref, q_ref, k_ref, v_ref):
    tab, ckvn, k_rope, q_nope, q_rope = _mla_common(cq_ref, ckv_ref, gate_ref, tab_ref, gq_ref, gkv_ref,
                                                    wqn_ref, wqr_ref, ckvn_ref, kr_ref)
    cb = ckvn.astype(BF16)
    k_nope = _dot(cb, wk_ref[...])
    v_ref[...] = _dot(cb, wv_ref[...]).astype(v_ref.dtype)
    krb = k_rope.astype(k_ref.dtype)
    for h in range(H_MLA):
        sl = slice(h * HEAD_DIM, (h + 1) * HEAD_DIM)
        lo = slice(2 * h * LANES, (2 * h + 1) * LANES)
        hi = slice((2 * h + 1) * LANES, (2 * h + 2) * LANES)
        q_ref[:, lo] = (q_nope[:, sl] * ATT_SCALE).astype(q_ref.dtype)
        q_ref[:, hi] = (_rope64(q_rope[:, sl], tab) * ATT_SCALE).astype(q_ref.dtype)
        k_ref[:, lo] = k_nope[:, sl].astype(k_ref.dtype)
        k_ref[:, hi] = krb


def _mla_prep_sample_kernel(cq_ref, ckv_ref, gate_ref, tab_ref, gq_ref, gkv_ref, wqn_ref, wqr_ref, wukt_ref,
                            ckvn_ref, kr_ref, qs_ref):
    tab, ckvn, k_rope, q_nope, q_rope = _mla_common(cq_ref, ckv_ref, gate_ref, tab_ref, gq_ref, gkv_ref,
                                                    wqn_ref, wqr_ref, ckvn_ref, kr_ref)
    for h in range(H_MLA):
        sl = slice(h * HEAD_DIM, (h + 1) * HEAD_DIM)
        q_lat = _dot(q_nope[:, sl].astype(BF16), wukt_ref[h])
        qs_ref[h, :, 0:KV_RANK] = q_lat * ATT_SCALE
        qs_ref[h, :, KV_RANK:KV_RANK + LANES] = _rope64(q_rope[:, sl], tab) * ATT_SCALE


def _mla_prep(proj, tab, gq, gkv, wqn, wqr, extra_w, layer, sample):
    T = proj.shape[0]
    tm = min(T, 512)
    tab_blocks = tab.shape[1] // tm
    full = lambda a: pl.BlockSpec(a.shape, lambda i: (0,) * a.ndim)
    of_layer = lambda a: pl.BlockSpec((None,) + a.shape[1:], lambda i: (layer,) + (0,) * (a.ndim - 1))
    in_specs = [pl.BlockSpec((tm, Q_RANK), lambda i: (i, COL_CQ * LANES // Q_RANK)),
                pl.BlockSpec((tm, KV_RANK), lambda i: (i, COL_CKV * LANES // KV_RANK)),
                pl.BlockSpec((tm, LANES), lambda i: (i, COL_GATE)),
                pl.BlockSpec((3, tm, LANES), lambda i: (0, i % tab_blocks, 0)),
                full(gq), full(gkv), of_layer(wqn), of_layer(wqr)] + [of_layer(w) for w in extra_w]
    out_specs = [pl.BlockSpec((tm, KV_RANK), lambda i: (i, 0)), pl.BlockSpec((tm, D_ROPE), lambda i: (i, 0))]
    out_shape = [jax.ShapeDtypeStruct((T, KV_RANK), F32), jax.ShapeDtypeStruct((T, D_ROPE), F32)]
    if sample:
        kern = _mla_prep_sample_kernel
        out_specs.append(pl.BlockSpec((H_MLA, tm, KV_RANK + LANES), lambda i: (0, i, 0)))
        out_shape.append(jax.ShapeDtypeStruct((H_MLA, T, KV_RANK + LANES), F32))
    else:
        kern = _mla_prep_prompt_kernel
        out_specs += [pl.BlockSpec((tm, 2 * D_MLA), lambda i: (i, 0)), pl.BlockSpec((tm, 2 * D_MLA), lambda i: (i, 0)),
                      pl.BlockSpec((tm, D_MLA), lambda i: (i, 0))]
        out_shape += [jax.ShapeDtypeStruct((T, 2 * D_MLA), BF16), jax.ShapeDtypeStruct((T, 2 * D_MLA), BF16),
                      jax.ShapeDtypeStruct((T, D_MLA), BF16)]
    return pl.pallas_call(
        kern, grid=(T // tm,), in_specs=in_specs, out_specs=out_specs, out_shape=out_shape,
        compiler_params=_cparams("parallel"),
        name="mla_prep_sample" if sample else "mla_prep_prompt")(proj, proj, proj, tab, gq, gkv, wqn, wqr, *extra_w)


def _attn_prompt_kernel(q_ref, k_ref, v_ref, z_ref, o_ref, *, tq):
    L = q_ref.shape[0]
    row = lax.broadcasted_iota(jnp.int32, (tq, tq), 0)
    col = lax.broadcasted_iota(jnp.int32, (tq, tq), 1)
    for i in range(L // tq):
        cur = slice(i * tq, (i + 1) * tq)
        q = q_ref[cur, :]
        sd = jnp.where(col <= row, _dot_nt(q, k_ref[cur, :]), -jnp.inf)
        m = jnp.max(sd, axis=1, keepdims=True)
        if i > 0:
            so = _dot_nt(q, k_ref[0:i * tq, :])
            m = jnp.maximum(m, jnp.max(so, axis=1, keepdims=True))
            po = jnp.exp(so - m)
            l = jnp.sum(po, axis=1, keepdims=True)
            acc = _dot(po.astype(BF16), v_ref[0:i * tq, :])
        pd = jnp.exp(sd - m)
        if i > 0:
            l = l + jnp.sum(pd, axis=1, keepdims=True)
            acc = acc + _dot(pd.astype(BF16), v_ref[cur, :])
        else:
            l = jnp.sum(pd, axis=1, keepdims=True)
            acc = _dot(pd.astype(BF16), v_ref[cur, :])
        o_ref[cur, :] = (acc / l * _silu(z_ref[cur, :])).astype(o_ref.dtype)


def _attn_prompt(q, k, v, proj, B, L):
    tq = min(L, 512)
    return pl.pallas_call(
        functools.partial(_attn_prompt_kernel, tq=tq), grid=(B, H_MLA),
        in_specs=[pl.BlockSpec((L, 2 * LANES), lambda b, h: (b, h)),
                  pl.BlockSpec((L, 2 * LANES), lambda b, h: (b, h)),
                  pl.BlockSpec((L, HEAD_DIM), lambda b, h: (b, h)),
                  pl.BlockSpec((L, HEAD_DIM), lambda b, h: (b, COL_AZ + h))],
        out_specs=pl.BlockSpec((L, HEAD_DIM), lambda b, h: (b, h)),
        out_shape=jax.ShapeDtypeStruct((B * L, D_MLA), BF16),
        compiler_params=_cparams("parallel", "parallel"), name="attn_prompt")(q, k, v, proj)


DECODE_SLOTS = 3


def _decode_kernel(pt_ref, qs_ref, cn_ref, kn_ref, ckv_hbm, krt_hbm, o_ref, ckv_buf, krt_buf, sem,
                   *, layer, n_pages, page, n_seq):
    i = pl.program_id(0)

    def page_copies(seq, slot):
        out = []
        for p in range(n_pages):
            pg = pt_ref[seq, p]
            rows = pl.ds(p * page, page)
            out.append(pltpu.make_async_copy(ckv_hbm.at[layer, pg], ckv_buf.at[slot, rows, :], sem.at[0, slot]))
            out.append(pltpu.make_async_copy(krt_hbm.at[layer, pg], krt_buf.at[slot, :, rows], sem.at[1, slot]))
        return out

    def start_fetch(seq, slot):
        for cp in page_copies(seq, slot):
            cp.start()

    def wait_fetch(seq, slot):
        for cp in page_copies(seq, slot):
            cp.wait()

    ahead = DECODE_SLOTS - 1

    @pl.when(i == 0)
    def _():
        for j in range(min(ahead, n_seq)):
            start_fetch(j, j)

    @pl.when(i + ahead < n_seq)
    def _():
        start_fetch(i + ahead, lax.rem(i + ahead, DECODE_SLOTS))

    slot = lax.rem(i, DECODE_SLOTS)
    wait_fetch(i, slot)

    q = qs_ref[0]
    q16 = jnp.concatenate([q, jnp.zeros_like(q)], axis=0).astype(BF16)
    ql = q16[:, 0:KV_RANK]
    qr = q16[:, KV_RANK:KV_RANK + D_ROPE]
    kc = ckv_buf[slot].astype(BF16)
    sc = _dot_nt(ql, kc) + _dot(qr, krt_buf[slot].astype(BF16))
    cn = cn_ref[0].astype(BF16).astype(F32)
    kn = kn_ref[0].astype(BF16).astype(F32)
    s_new = (jnp.sum(ql.astype(F32) * cn, axis=1, keepdims=True)
             + jnp.sum(qr.astype(F32) * kn, axis=1, keepdims=True))
    m = jnp.maximum(jnp.max(sc, axis=1, keepdims=True), s_new)
    p = jnp.exp(sc - m)
    p_new = jnp.exp(s_new - m)
    l = jnp.sum(p, axis=1, keepdims=True) + p_new
    pb = p.astype(BF16)
    half = kc.shape[0] // 2
    acc = (_dot(pb[:, :half], kc[:half]) + _dot(pb[:, half:], kc[half:])
           + p_new.astype(BF16).astype(F32) * cn)
    o_ref[0] = (acc / l)[0:H_MLA, :]


def _decode(page_table, qs, ckv_new, kr_new, cache_ckv, cache_krope_t, layer):
    DB, n_pages = page_table.shape
    page = cache_ckv.shape[2]
    P = n_pages * page
    kern = functools.partial(_decode_kernel, layer=layer, n_pages=n_pages, page=page, n_seq=DB)
    grid_spec = pltpu.PrefetchScalarGridSpec(
        num_scalar_prefetch=1, grid=(DB,),
        in_specs=[pl.BlockSpec((1, H_MLA, KV_RANK + LANES), lambda b, pt: (b, 0, 0)),
                  pl.BlockSpec((1, 1, KV_RANK), lambda b, pt: (b, 0, 0)),
                  pl.BlockSpec((1, 1, D_ROPE), lambda b, pt: (b, 0, 0)),
                  pl.BlockSpec(memory_space=pl.ANY), pl.BlockSpec(memory_space=pl.ANY)],
        out_specs=pl.BlockSpec((1, H_MLA, KV_RANK), lambda b, pt: (b, 0, 0)),
        scratch_shapes=[pltpu.VMEM((DECODE_SLOTS, P, KV_RANK), F32), pltpu.VMEM((DECODE_SLOTS, D_ROPE, P), F32),
                        pltpu.SemaphoreType.DMA((2, DECODE_SLOTS))])
    return pl.pallas_call(
        kern, grid_spec=grid_spec, out_shape=jax.ShapeDtypeStruct((DB, H_MLA, KV_RANK), F32),
        compiler_params=_cparams("arbitrary"), name="mla_decode")(
            page_table, qs, ckv_new.reshape(DB, 1, KV_RANK), kr_new.reshape(DB, 1, D_ROPE), cache_ckv, cache_krope_t)


def _mla_out_kernel(ol_ref, w_ref, z_ref, o_ref):
    o = _dot(ol_ref[0].astype(BF16), w_ref[...])
    o_ref[...] = (o * _silu(z_ref[...])).astype(o_ref.dtype)


def _mla_out(o_lat_h, wuv, layer, proj):
    T = proj.shape[0]
    return pl.pallas_call(
        _mla_out_kernel, grid=(H_MLA,),
        in_specs=[pl.BlockSpec((1, T, KV_RANK), lambda h: (h, 0, 0)),
                  pl.BlockSpec((None, None, KV_RANK, HEAD_DIM), lambda h: (layer, h, 0, 0)),
                  pl.BlockSpec((T, HEAD_DIM), lambda h: (0, COL_AZ + h))],
        out_specs=pl.BlockSpec((T, HEAD_DIM), lambda h: (0, h)),
        out_shape=jax.ShapeDtypeStruct((T, D_MLA), BF16),
        compiler_params=_cparams("parallel"), name="mla_out")(o_lat_h, wuv, proj)


def _out_proj_kernel(or_ref, om_ref, oa_ref, x_ref, w_ref, g_ref, *out_refs, last):
    x = (x_ref[...] + _dot(or_ref[...], w_ref[0:D_RET, :]) + _dot(om_ref[...], w_ref[D_RET:D_RET + D_ML, :])
         + _dot(oa_ref[...], w_ref[D_RET + D_ML:, :]))
    hn = _rms(x, g_ref[...])
    if last:
        out_refs[0][...] = hn
    else:
        out_refs[0][...] = x
        out_refs[1][...] = hn.astype(out_refs[1].dtype)


def _out_proj(o_ret, o_ml, o_mla, x, w, layer, g, last):
    T, D = x.shape
    tm = min(T, 512)
    row = lambda n: pl.BlockSpec((tm, n), lambda i: (i, 0))
    if last:
        out_specs, out_shape = [row(D)], [jax.ShapeDtypeStruct((T, D), F32)]
    else:
        out_specs = [row(D), row(D)]
        out_shape = [jax.ShapeDtypeStruct((T, D), F32), jax.ShapeDtypeStruct((T, D), BF16)]
    return pl.pallas_call(
        functools.partial(_out_proj_kernel, last=last), grid=(T // tm,),
        in_specs=[row(D_RET), row(D_ML), row(D_MLA), row(D),
                  pl.BlockSpec((None,) + w.shape[1:], lambda i: (layer, 0, 0)),
                  pl.BlockSpec((1, D), lambda i: (0, 0))],
        out_specs=out_specs, out_shape=out_shape,
        compiler_params=_cparams("parallel"), name="out_proj")(o_ret, o_ml, o_mla, x, w, g.reshape(1, D))


def _rope_tables(pos):
    pos = pos.astype(F32)[:, None]

    def cs(half):
        inv = ROPE_BASE ** (-jnp.arange(half, dtype=F32) / half)
        ang = pos * inv[None, :]
        return jnp.cos(ang), jnp.sin(ang)

    c64, s64 = cs(HEAD_DIM // 2)
    tab128 = jnp.stack([jnp.concatenate([c64, c64], -1), jnp.concatenate([-s64, s64], -1)])
    c32, s32 = cs(D_ROPE // 2)
    z32 = jnp.zeros_like(c32)
    tab64 = jnp.stack([jnp.concatenate([c32, c32, z32, z32], -1),
                       jnp.concatenate([-s32, z32, z32, z32], -1),
                       jnp.concatenate([z32, s32, z32, z32], -1)])
    return tab128, tab64


def _layout_w_in(w_in):
    wt = jnp.swapaxes(w_in, 1, 2)
    o = np.cumsum((0,) + IN_WIDTHS)
    seg = lambda a, b: wt[:, o[a]:o[b], :]
    pad = jnp.zeros((wt.shape[0], N_IN_PAD - D_IN, wt.shape[2]), wt.dtype)
    return jnp.concatenate([seg(0, 9), seg(11, 12), seg(12, 13), seg(14, 15), seg(13, 14), seg(9, 11), pad],
                           axis=1).astype(BF16)


def kernel(x_prompt, x_sample, state_ret, state_mlstm_c, state_mlstm_n, state_mlstm_m, cache_ckv, cache_krope,
           page_table, g_norm, w_in, b_ig, b_fg, g_ret, g_ml, g_q, g_kv, w_uq, w_uk, w_uv, w_out, g_final):
    B, L, D = x_prompt.shape
    DB, Ls, _ = x_sample.shape
    assert Ls == 1, "the sample group is one new token per sequence"
    depth = w_in.shape[0]
    past_len = page_table.shape[1] * cache_ckv.shape[2]

    tab128_p, tab64_p = _rope_tables(jnp.arange(L))
    tab128_s, tab64_s = _rope_tables(past_len + jnp.arange(Ls))
    tab128_s = jnp.broadcast_to(tab128_s, (2, DB, HEAD_DIM))
    tab64_s = jnp.broadcast_to(tab64_s, (3, DB, LANES))
    w_in_b = _layout_w_in(w_in)
    w_out_b = w_out.astype(BF16)
    w_uq_t = jnp.transpose(w_uq, (0, 2, 3, 1))
    wqn = w_uq_t[:, :, :D_NOPE, :].reshape(depth, D_MLA, Q_RANK).astype(BF16)
    wqr = jnp.concatenate([w_uq_t[:, :, D_NOPE:, :], jnp.zeros((depth, H_MLA, LANES - D_ROPE, Q_RANK), w_uq.dtype)],
                          2).reshape(depth, D_MLA, Q_RANK).astype(BF16)
    wk = w_uk.reshape(depth, KV_RANK, D_MLA).astype(BF16)
    wv = w_uv.reshape(depth, KV_RANK, D_MLA).astype(BF16)
    wukt = jnp.transpose(w_uk, (0, 2, 3, 1)).astype(BF16)
    wuv_h = jnp.transpose(w_uv, (0, 2, 1, 3)).astype(BF16)
    zeros_gate = jnp.zeros((depth, LANES - D_ROPE - 2 * H_ML), F32)
    gate_bias = jnp.concatenate([jnp.zeros((depth, D_ROPE), F32), b_ig, b_fg, zeros_gate], -1)

    cache_krope_t = jnp.swapaxes(cache_krope, 2, 3)

    xp = x_prompt.reshape(B * L, D)
    xs = x_sample.reshape(DB, D)
    hp = _rmsnorm(xp, g_norm[0], BF16)
    hs = _rmsnorm(xs, g_norm[0], BF16)
    p_out = [[] for _ in range(6)]
    s_out = [[] for _ in range(4)]
    s_ret_all = s_c_all = None
    for l in range(depth):
        last = l == depth - 1
        g_next = g_final if last else g_norm[l + 1]
        gq, gkv = g_q[l].reshape(1, Q_RANK), g_kv[l].reshape(1, KV_RANK)
        gr, gm = g_ret[l].reshape(1, D_RET), g_ml[l].reshape(1, D_ML)
        gb = gate_bias[l].reshape(1, LANES)

        proj = _in_proj(hp, w_in_b, l)
        o_ret, st_ret = _ret_prompt(proj, tab128_p, gr, B, L)
        o_ml, st_c, st_n, st_m = _mlstm_prompt(proj, gb, gm, B, L)
        ckvn, krope, q, k, v = _mla_prep(proj, tab64_p, gq, gkv, wqn, wqr, (wk, wv), l, sample=False)
        o_mla = _attn_prompt(q, k, v, proj, B, L)
        res = _out_proj(o_ret, o_ml, o_mla, xp, w_out_b, l, g_next, last)
        if last:
            y_prompt = res[0]
        else:
            xp, hp = res
        for j, t in enumerate((st_ret, st_c, st_n[:, :, 0, :], st_m[:, :, 0, 0],
                               ckvn.reshape(B, L, KV_RANK), krope.reshape(B, L, D_ROPE))):
            p_out[j].append(t)

        proj_s = _in_proj(hs, w_in_b, l)
        o_ret, s_ret_all = _ret_step(proj_s, tab128_s, gr, state_ret, s_ret_all, l)
        o_ml, s_c_all, st_n, st_m = _mlstm_step(proj_s, gb, gm, state_mlstm_c, state_mlstm_n, state_mlstm_m,
                                                s_c_all, l)
        ckvn, krope, qs = _mla_prep(proj_s, tab64_s, gq, gkv, wqn, wqr, (wukt,), l, sample=True)
        o_lat = _decode(page_table, jnp.transpose(qs, (1, 0, 2)), ckvn, krope, cache_ckv, cache_krope_t, l)
        o_mla = _mla_out(jnp.transpose(o_lat, (1, 0, 2)), wuv_h, l, proj_s)
        res = _out_proj(o_ret, o_ml, o_mla, xs, w_out_b, l, g_next, last)
        if last:
            y_sample = res[0]
        else:
            xs, hs = res
        for j, t in enumerate((st_n, st_m.reshape(DB, H_ML, HEAD_DIM)[:, :, 0],
                               ckvn.reshape(DB, Ls, KV_RANK), krope.reshape(DB, Ls, D_ROPE))):
            s_out[j].append(t)

    return (y_prompt.reshape(B, L, D), y_sample.reshape(DB, Ls, D),
            *[jnp.stack(t) for t in p_out], s_ret_all, s_c_all, *[jnp.stack(t) for t in s_out])
```

```python
import functools

import numpy as np
import jax
import jax.numpy as jnp
from jax import lax
from jax.experimental import pallas as pl
from jax.experimental.pallas import tpu as pltpu

F32 = jnp.float32
BF16 = jnp.bfloat16

HEAD_DIM = 128
H_RET = 4
H_ML = 4
H_MLA = 8
D_RET = H_RET * HEAD_DIM
D_ML = H_ML * HEAD_DIM
D_MLA = H_MLA * HEAD_DIM
Q_RANK = 512
KV_RANK = 256
D_NOPE = 128
D_ROPE = 64
ROPE_BASE = 10000.0
MIX_CHUNK = 128
NORM_EPS = 1e-6
MLSTM_EPS = 1e-6
ATT_SCALE = (D_NOPE + D_ROPE) ** -0.5
QK_SCALE = HEAD_DIM ** -0.5
IN_WIDTHS = (D_RET, D_RET, D_RET, D_RET, D_ML, D_ML, D_ML, D_ML, D_ML, H_ML, H_ML,
             Q_RANK, KV_RANK, D_ROPE, D_MLA)
D_IN = sum(IN_WIDTHS)

LANES = 128
VMEM_LIMIT_BYTES = 56 * 1024 * 1024

COL_RET = 0
COL_ML = 16
N_MAIN = 36 * LANES
COL_CQ = 0
COL_CKV = 4
COL_AZ = 6
COL_GATE = 14
GATE_I_LANE = D_ROPE
GATE_F_LANE = D_ROPE + H_ML
N_TAIL = 16 * LANES
RET_LOG_G = [float(np.log1p(-np.exp2(np.float32(-5.0 - h)))) for h in range(H_RET)]


def _cparams(*sem):
    return pltpu.CompilerParams(dimension_semantics=sem, vmem_limit_bytes=VMEM_LIMIT_BYTES)


def _dot(a, b):
    return jnp.dot(a, b, preferred_element_type=F32)


def _dot_nt(a, b):
    return lax.dot_general(a, b, (((1,), (1,)), ((), ())), preferred_element_type=F32)


def _silu(x):
    return x * jax.nn.sigmoid(x)


def _rms(x, g):
    return x * lax.rsqrt(jnp.mean(x * x, axis=-1, keepdims=True) + NORM_EPS) * g


def _head_norm(o, g):
    mu = jnp.mean(o, axis=-1, keepdims=True)
    d = o - mu
    var = jnp.mean(d * d, axis=-1, keepdims=True)
    return d * lax.rsqrt(var + NORM_EPS) * g


def _rope128(x, cos, sin_signed):
    return x * cos + pltpu.roll(x, HEAD_DIM // 2, axis=1) * sin_signed


def _rope128b(x, cos, sin_signed):
    return x * cos + pltpu.roll(x, HEAD_DIM // 2, axis=2) * sin_signed


def _rope64(x, tab):
    q = D_ROPE // 2
    return x * tab[0] + pltpu.roll(x, LANES - q, axis=1) * tab[1] + pltpu.roll(x, q, axis=1) * tab[2]


def _rmsnorm_kernel(x_ref, g_ref, o_ref):
    o_ref[...] = _rms(x_ref[...], g_ref[...]).astype(o_ref.dtype)


def _rmsnorm(x, g, dtype):
    T, D = x.shape
    tm = min(T, 512)
    return pl.pallas_call(
        _rmsnorm_kernel, grid=(T // tm,),
        in_specs=[pl.BlockSpec((tm, D), lambda i: (i, 0)), pl.BlockSpec((1, D), lambda i: (0, 0))],
        out_specs=pl.BlockSpec((tm, D), lambda i: (i, 0)),
        out_shape=jax.ShapeDtypeStruct((T, D), dtype),
        compiler_params=_cparams("parallel"), name="rmsnorm")(x, g.reshape(1, D))


def _matmul_nt_kernel(h_ref, wt_ref, o_ref):
    o_ref[...] = _dot_nt(h_ref[...], wt_ref[...].astype(BF16))


def _in_proj(hn, wt, layer, N):
    T, D = hn.shape
    tm = min(T, 2048)
    tn = 4 * LANES
    return pl.pallas_call(
        _matmul_nt_kernel, grid=(T // tm, N // tn),
        in_specs=[pl.BlockSpec((tm, D), lambda i, j: (i, 0)),
                  pl.BlockSpec((None, tn, D), lambda i, j: (layer, j, 0))],
        out_specs=pl.BlockSpec((tm, tn), lambda i, j: (i, j)),
        out_shape=jax.ShapeDtypeStruct((T, N), F32),
        compiler_params=_cparams("parallel", "arbitrary"), name="in_proj")(hn, wt)


def _ret_prompt_kernel(q_ref, k_ref, v_ref, z_ref, cs_ref, g_ref, o_ref, s_ref, S_scr):
    ci = pl.program_id(0)
    B, c = q_ref.shape[0], q_ref.shape[1]

    @pl.when(ci == 0)
    def _():
        S_scr[...] = jnp.zeros_like(S_scr)

    cos = cs_ref[0]
    sin = cs_ref[1]
    row = lax.broadcasted_iota(jnp.int32, (c, c), 0)
    col = lax.broadcasted_iota(jnp.int32, (c, c), 1)
    rel = (row - col).astype(F32)
    idx = lax.broadcasted_iota(jnp.int32, (c, 1), 0).astype(F32)
    for h in range(H_RET):
        sl = slice(h * HEAD_DIM, (h + 1) * HEAD_DIM)
        lg = RET_LOG_G[h]
        decay = jnp.where(rel >= 0, jnp.exp(lg * jnp.maximum(rel, 0.0)), 0.0)
        q_dec = jnp.exp(lg * (idx + 1.0))
        k_dec = jnp.exp(lg * (c - 1.0 - idx))
        bmm = lambda e, x, y: jnp.einsum(e, x, y, preferred_element_type=F32)
        q = _rope128b(q_ref[:, :, sl], cos, sin)
        k = _rope128b(k_ref[:, :, sl], cos, sin) * QK_SCALE
        vb = v_ref[:, :, sl].astype(BF16)
        qb = q.astype(BF16)
        S = S_scr[:, h]
        sc = bmm('bik,bjk->bij', qb, k.astype(BF16)) * decay
        o = bmm('bij,bjv->biv', sc.astype(BF16), vb) + bmm('bik,bkv->biv', qb, S.astype(BF16)) * q_dec
        kdt = jnp.swapaxes(k * k_dec, 1, 2).astype(BF16)
        S_scr[:, h] = S * float(np.exp(lg * c)) + bmm('bkj,bjv->bkv', kdt, vb)
        y = _head_norm(o, g_ref[:, sl]) * _silu(z_ref[:, :, sl])
        o_ref[:, :, sl] = y.astype(o_ref.dtype)

    @pl.when(ci == pl.num_programs(0) - 1)
    def _():
        s_ref[...] = S_scr[...]


def _ret_prompt(proj, cs, g, B, L):
    c = MIX_CHUNK if L % MIX_CHUNK == 0 else L
    wb = D_RET // LANES
    col = lambda j: (lambda i: (0, i, COL_RET // wb + j))
    proj3 = proj.reshape(B, L, proj.shape[1])
    o, st = pl.pallas_call(
        _ret_prompt_kernel, grid=(L // c,),
        in_specs=[pl.BlockSpec((B, c, D_RET), col(0)), pl.BlockSpec((B, c, D_RET), col(1)),
                  pl.BlockSpec((B, c, D_RET), col(2)), pl.BlockSpec((B, c, D_RET), col(3)),
                  pl.BlockSpec((2, c, HEAD_DIM), lambda i: (0, i, 0)),
                  pl.BlockSpec((1, D_RET), lambda i: (0, 0))],
        out_specs=[pl.BlockSpec((B, c, D_RET), lambda i: (0, i, 0)),
                   pl.BlockSpec((B, H_RET, HEAD_DIM, HEAD_DIM), lambda i: (0, 0, 0, 0))],
        out_shape=[jax.ShapeDtypeStruct((B, L, D_RET), BF16),
                   jax.ShapeDtypeStruct((B, H_RET, HEAD_DIM, HEAD_DIM), F32)],
        scratch_shapes=[pltpu.VMEM((B, H_RET, HEAD_DIM, HEAD_DIM), F32)],
        compiler_params=_cparams("arbitrary"), name="ret_prompt")(proj3, proj3, proj3, proj3, cs, g)
    return o.reshape(B * L, D_RET), st


def _log_sigmoid(x):
    return -(jnp.maximum(-x, 0.0) + jnp.log1p(jnp.exp(-jnp.abs(x))))


def _mlstm_prompt_kernel(q_ref, k_ref, v_ref, og_ref, z_ref, gate_ref, bias_ref, g_ref,
                         o_ref, c_ref, n_ref, m_ref, C_scr, M_scr):
    ci = pl.program_id(0)
    B, c = q_ref.shape[0], q_ref.shape[1]

    @pl.when(ci == 0)
    def _():
        C_scr[...] = jnp.zeros_like(C_scr)
        M_scr[...] = jnp.zeros_like(M_scr)

    row = lax.broadcasted_iota(jnp.int32, (c, c), 0)
    col = lax.broadcasted_iota(jnp.int32, (c, c), 1)
    causal = col <= row
    eye = col == row
    ones = jnp.ones((B, c, HEAD_DIM), BF16)
    gates = gate_ref[...] + bias_ref[...]
    bmm = lambda e, x, y: jnp.einsum(e, x, y, preferred_element_type=F32)
    for h in range(H_ML):
        sl = slice(h * HEAD_DIM, (h + 1) * HEAD_DIM)
        q = q_ref[:, :, sl]
        k = k_ref[:, :, sl] * QK_SCALE
        vb1 = jnp.concatenate([v_ref[:, :, sl].astype(BF16), ones], axis=2)
        i_col = gates[:, :, GATE_I_LANE + h:GATE_I_LANE + h + 1]
        lf_col = _log_sigmoid(gates[:, :, GATE_F_LANE + h:GATE_F_LANE + h + 1])
        lf_row = jnp.sum(jnp.where(eye, lf_col, 0.0), axis=1, keepdims=True)
        i_row = jnp.sum(jnp.where(eye, i_col, 0.0), axis=1, keepdims=True)
        b_col = jnp.sum(jnp.where(causal, lf_row, 0.0), axis=2, keepdims=True)
        b_row = jnp.sum(jnp.where(row <= col, lf_col, 0.0), axis=1, keepdims=True)
        m_prev = M_scr[:, h][:, :, 0:1]
        log_d = jnp.where(causal, b_col - b_row + i_row, -jnp.inf)
        inter = b_col + m_prev
        m_t = jnp.maximum(inter, jnp.max(log_d, axis=2, keepdims=True))
        w_intra = jnp.exp(log_d - m_t)
        w_inter = jnp.exp(inter - m_t)
        qb = q.astype(BF16)
        C1 = C_scr[:, h]
        s = bmm('bik,bjk->bij', qb, k.astype(BF16)) * w_intra
        a = bmm('bij,bjv->biv', s.astype(BF16), vb1) + w_inter * bmm('bik,bkv->biv', qb, C1.astype(BF16))
        hb = a[:, :, :HEAD_DIM] / (jnp.maximum(jnp.abs(a[:, :, HEAD_DIM:]), jnp.exp(-m_t)) + MLSTM_EPS)
        m_new = m_t[:, c - 1:c, :]
        b_last = b_col[:, c - 1:c, :]
        w_k = jnp.exp(b_last - b_col + i_col - m_new)
        dec = jnp.exp(b_last + m_prev - m_new)
        kwt = jnp.swapaxes(k * w_k, 1, 2).astype(BF16)
        C_scr[:, h] = dec * C1 + bmm('bkj,bjv->bkv', kwt, vb1)
        M_scr[:, h] = jnp.broadcast_to(m_new, (B, 1, HEAD_DIM))
        y = _head_norm(hb * jax.nn.sigmoid(og_ref[:, :, sl]), g_ref[:, sl]) * _silu(z_ref[:, :, sl])
        o_ref[:, :, sl] = y.astype(o_ref.dtype)

    @pl.when(ci == pl.num_programs(0) - 1)
    def _():
        m_ref[...] = M_scr[...]
        for b in range(B):
            for h in range(H_ML):
                C1 = C_scr[b, h]
                c_ref[b, h] = C1[:, :HEAD_DIM]
                n_ref[b, h] = C1[:, HEAD_DIM:].T[0:1, :]


def _mlstm_prompt(proj, proj_tail, gate_bias, g, B, L):
    c = MIX_CHUNK if L % MIX_CHUNK == 0 else L
    wb = D_ML // LANES
    col = lambda j: (lambda i: (0, i, COL_ML // wb + j))
    proj3 = proj.reshape(B, L, proj.shape[1])
    tail3 = proj_tail.reshape(B, L, proj_tail.shape[1])
    vec = jax.ShapeDtypeStruct((B, H_ML, 1, HEAD_DIM), F32)
    vec_spec = pl.BlockSpec((B, H_ML, 1, HEAD_DIM), lambda i: (0, 0, 0, 0))
    o, st_c, st_n, st_m = pl.pallas_call(
        _mlstm_prompt_kernel, grid=(L // c,),
        in_specs=[pl.BlockSpec((B, c, D_ML), col(0)), pl.BlockSpec((B, c, D_ML), col(1)),
                  pl.BlockSpec((B, c, D_ML), col(2)), pl.BlockSpec((B, c, D_ML), col(3)),
                  pl.BlockSpec((B, c, D_ML), col(4)),
                  pl.BlockSpec((B, c, LANES), lambda i: (0, i, COL_GATE)),
                  pl.BlockSpec((1, LANES), lambda i: (0, 0)),
                  pl.BlockSpec((1, D_ML), lambda i: (0, 0))],
        out_specs=[pl.BlockSpec((B, c, D_ML), lambda i: (0, i, 0)),
                   pl.BlockSpec((B, H_ML, HEAD_DIM, HEAD_DIM), lambda i: (0, 0, 0, 0)),
                   vec_spec, vec_spec],
        out_shape=[jax.ShapeDtypeStruct((B, L, D_ML), BF16),
                   jax.ShapeDtypeStruct((B, H_ML, HEAD_DIM, HEAD_DIM), F32), vec, vec],
        scratch_shapes=[pltpu.VMEM((B, H_ML, HEAD_DIM, 2 * HEAD_DIM), F32),
                        pltpu.VMEM((B, H_ML, 1, HEAD_DIM), F32)],
        compiler_params=_cparams("arbitrary"),
        name="mlstm_prompt")(proj3, proj3, proj3, proj3, proj3, tail3, gate_bias, g)
    return o.reshape(B * L, D_ML), st_c, st_n, st_m


STEP_TOKENS = 16


def _outer_t(kb, vm):
    return lax.dot_general(kb, vm, (((0,), (0,)), ((), ())), preferred_element_type=F32)


def _ret_step_kernel(q_ref, k_ref, v_ref, z_ref, cs_ref, g_ref, s_ref, *rest):
    o_ref, so_ref = rest[-2:]
    tb = q_ref.shape[0]
    cos = cs_ref[0]
    sin = cs_ref[1]
    tok = lax.broadcasted_iota(jnp.int32, (tb, HEAD_DIM), 0)
    for h in range(H_RET):
        sl = slice(h * HEAD_DIM, (h + 1) * HEAD_DIM)
        gamma = float(np.exp(RET_LOG_G[h]))
        q = _rope128(q_ref[:, sl], cos, sin)
        k = _rope128(k_ref[:, sl], cos, sin) * QK_SCALE
        v = v_ref[:, sl]
        qb = q.astype(BF16)
        kb = k.astype(BF16)
        ro = jnp.zeros((tb, HEAD_DIM), F32)
        for t in range(tb):
            S = s_ref[0, t, h]
            ro = jnp.where(tok == t, _dot(qb, S.astype(BF16)), ro)
            vm = jnp.where(tok == t, v, 0.0).astype(BF16)
            so_ref[0, t, h] = gamma * S + _outer_t(kb, vm)
        o = jnp.sum(q * k, axis=1, keepdims=True) * v + gamma * ro
        y = _head_norm(o, g_ref[:, sl]) * _silu(z_ref[:, sl])
        o_ref[:, sl] = y.astype(o_ref.dtype)


def _stacked_state_specs(prev, depth, T, tb, H, layer):
    shape = jax.ShapeDtypeStruct((depth, T, H, HEAD_DIM, HEAD_DIM), F32)
    spec = pl.BlockSpec((1, tb, H, HEAD_DIM, HEAD_DIM), lambda i: (layer, i, 0, 0, 0))
    if prev is None:
        prev = jnp.zeros(shape.shape, shape.dtype)
    return shape, spec, [pl.BlockSpec(memory_space=pl.ANY)], [prev]


def _ret_step(proj, cs, g, state, prev, layer):
    T = proj.shape[0]
    depth = state.shape[0]
    tb = min(T, STEP_TOKENS)
    wb = D_RET // LANES
    col = lambda j: (lambda i: (i, COL_RET // wb + j))
    st_shape, st_spec, extra_specs, extra_args = _stacked_state_specs(prev, depth, T, tb, H_RET, layer)
    n_in = 7
    return pl.pallas_call(
        _ret_step_kernel, grid=(T // tb,),
        in_specs=[pl.BlockSpec((tb, D_RET), col(0)), pl.BlockSpec((tb, D_RET), col(1)),
                  pl.BlockSpec((tb, D_RET), col(2)), pl.BlockSpec((tb, D_RET), col(3)),
                  pl.BlockSpec((2, tb, HEAD_DIM), lambda i: (0, i, 0)),
                  pl.BlockSpec((1, D_RET), lambda i: (0, 0)),
                  pl.BlockSpec((1, tb, H_RET, HEAD_DIM, HEAD_DIM), lambda i: (layer, i, 0, 0, 0))] + extra_specs,
        out_specs=[pl.BlockSpec((tb, D_RET), lambda i: (i, 0)), st_spec],
        out_shape=[jax.ShapeDtypeStruct((T, D_RET), BF16), st_shape],
        input_output_aliases={n_in: 1},
        compiler_params=_cparams("parallel"), name="ret_step")(proj, proj, proj, proj, cs, g, state, *extra_args)


def _mlstm_step_kernel(q_ref, k_ref, v_ref, og_ref, z_ref, gate_ref, bias_ref, g_ref, c_ref, n_ref, m_ref,
                       *rest):
    o_ref, co_ref, no_ref, mo_ref = rest[-4:]
    tb = q_ref.shape[0]
    gates = gate_ref[...] + bias_ref[...]
    m_all = m_ref[0]
    tok = lax.broadcasted_iota(jnp.int32, (tb, HEAD_DIM), 0)
    for h in range(H_ML):
        sl = slice(h * HEAD_DIM, (h + 1) * HEAD_DIM)
        q = q_ref[:, sl]
        k = k_ref[:, sl] * QK_SCALE
        v = v_ref[:, sl]
        ib = gates[:, GATE_I_LANE + h:GATE_I_LANE + h + 1]
        lf = _log_sigmoid(gates[:, GATE_F_LANE + h:GATE_F_LANE + h + 1])
        inter = lf + m_all[:, h:h + 1]
        m_t = jnp.maximum(inter, ib)
        w_intra = jnp.exp(ib - m_t)
        w_inter = jnp.exp(inter - m_t)
        nv = n_ref[0, :, h, :]
        qb = q.astype(BF16)
        kb = k.astype(BF16)
        vw = v * w_intra
        ro = jnp.zeros((tb, HEAD_DIM), F32)
        for t in range(tb):
            Cm = c_ref[0, t, h]
            ro = jnp.where(tok == t, _dot(qb, Cm.astype(BF16)), ro)
            vm = jnp.where(tok == t, vw, 0.0).astype(BF16)
            co_ref[0, t, h] = w_inter[t:t + 1, :] * Cm + _outer_t(kb, vm)
        s = jnp.sum(q * k, axis=1, keepdims=True) * w_intra
        num = s * v + w_inter * ro
        qn = s + w_inter * jnp.sum(q * nv, axis=1, keepdims=True)
        hb = num / (jnp.maximum(jnp.abs(qn), jnp.exp(-m_t)) + MLSTM_EPS)
        no_ref[:, h, :] = w_inter * nv + w_intra * k
        mo_ref[:, sl] = jnp.broadcast_to(m_t, (tb, HEAD_DIM))
        y = _head_norm(hb * jax.nn.sigmoid(og_ref[:, sl]), g_ref[:, sl]) * _silu(z_ref[:, sl])
        o_ref[:, sl] = y.astype(o_ref.dtype)


def _mlstm_step(proj, proj_tail, gate_bias, g, c_state, n_state, m_state, prev, layer):
    T = proj.shape[0]
    depth = c_state.shape[0]
    tb = min(T, STEP_TOKENS)
    wb = D_ML // LANES
    col = lambda j: (lambda i: (i, COL_ML // wb + j))
    st_shape, st_spec, extra_specs, extra_args = _stacked_state_specs(prev, depth, T, tb, H_ML, layer)
    n_in = 11
    return pl.pallas_call(
        _mlstm_step_kernel, grid=(T // tb,),
        in_specs=[pl.BlockSpec((tb, D_ML), col(0)), pl.BlockSpec((tb, D_ML), col(1)),
                  pl.BlockSpec((tb, D_ML), col(2)), pl.BlockSpec((tb, D_ML), col(3)),
                  pl.BlockSpec((tb, D_ML), col(4)),
                  pl.BlockSpec((tb, LANES), lambda i: (i, COL_GATE)),
                  pl.BlockSpec((1, LANES), lambda i: (0, 0)),
                  pl.BlockSpec((1, D_ML), lambda i: (0, 0)),
                  pl.BlockSpec((1, tb, H_ML, HEAD_DIM, HEAD_DIM), lambda i: (layer, i, 0, 0, 0)),
                  pl.BlockSpec((1, tb, H_ML, HEAD_DIM), lambda i: (layer, i, 0, 0)),
                  pl.BlockSpec((1, tb, H_ML), lambda i: (layer, i, 0))] + extra_specs,
        out_specs=[pl.BlockSpec((tb, D_ML), lambda i: (i, 0)), st_spec,
                   pl.BlockSpec((tb, H_ML, HEAD_DIM), lambda i: (i, 0, 0)),
                   pl.BlockSpec((tb, D_ML), lambda i: (i, 0))],
        out_shape=[jax.ShapeDtypeStruct((T, D_ML), BF16), st_shape,
                   jax.ShapeDtypeStruct((T, H_ML, HEAD_DIM), F32),
                   jax.ShapeDtypeStruct((T, D_ML), F32)],
        input_output_aliases={n_in: 1},
        compiler_params=_cparams("parallel"),
        name="mlstm_step")(proj, proj, proj, proj, proj, proj_tail, gate_bias, g, c_state, n_state, m_state,
                           *extra_args)


def _mla_common(cq_ref, ckv_ref, gate_ref, tab_ref, gq_ref, gkv_ref, wqn_ref, wqr_ref, ckvn_ref, kr_ref):
    tab = tab_ref[...]
    cqn = _rms(cq_ref[...], gq_ref[...]).astype(BF16)
    ckvn = _rms(ckv_ref[...], gkv_ref[...])
    ckvn_ref[...] = ckvn
    k_rope = _rope64(gate_ref[...], tab)
    kr_ref[...] = k_rope[:, :D_ROPE]
    q_nope = _dot_nt(cqn, wqn_ref[...])
    q_rope = _dot_nt(cqn, wqr_ref[...])
    return tab, ckvn, k_rope, q_nope, q_rope


def _mla_prep_prompt_kernel(cq_ref, ckv_ref, gate_ref, tab_ref, gq_ref, gkv_ref, wqn_ref, wqr_ref, wk_ref, wv_ref,
                            ckvn_ref, kr_ref, q_ref, k_ref, v_ref):
    tab, ckvn, k_rope, q_nope, q_rope = _mla_common(cq_ref, ckv_ref, gate_ref, tab_ref, gq_ref, gkv_ref,
                                                    wqn_ref, wqr_ref, ckvn_ref, kr_ref)
    cb = ckvn.astype(BF16)
    k_nope = _dot(cb, wk_ref[...])
    v_ref[...] = _dot(cb, wv_ref[...]).astype(v_ref.dtype)
    krb = k_rope.astype(k_ref.dtype)
    for h in range(H_MLA):
        sl = slice(h * HEAD_DIM, (h + 1) * HEAD_DIM)
        lo = slice(2 * h * LANES, (2 * h + 1) * LANES)
        hi = slice((2 * h + 1) * LANES, (2 * h + 2) * LANES)
        q_ref[:, lo] = (q_nope[:, sl] * ATT_SCALE).astype(q_ref.dtype)
        q_ref[:, hi] = (_rope64(q_rope[:, sl], tab) * ATT_SCALE).astype(q_ref.dtype)
        k_ref[:, lo] = k_nope[:, sl].astype(k_ref.dtype)
        k_ref[:, hi] = krb


def _mla_prep_sample_kernel(cq_ref, ckv_ref, gate_ref, tab_ref, gq_ref, gkv_ref, wqn_ref, wqr_ref, wukt_ref,
                            ckvn_ref, kr_ref, qs_ref):
    tab, ckvn, k_rope, q_nope, q_rope = _mla_common(cq_ref, ckv_ref, gate_ref, tab_ref, gq_ref, gkv_ref,
                                                    wqn_ref, wqr_ref, ckvn_ref, kr_ref)
    for h in range(H_MLA):
        sl = slice(h * HEAD_DIM, (h + 1) * HEAD_DIM)
        q_lat = _dot(q_nope[:, sl].astype(BF16), wukt_ref[h])
        qs_ref[h, :, 0:KV_RANK] = q_lat * ATT_SCALE
        qs_ref[h, :, KV_RANK:KV_RANK + LANES] = _rope64(q_rope[:, sl], tab) * ATT_SCALE


def _mla_prep(proj, tab, gq, gkv, wqn, wqr, extra_w, layer, sample):
    T = proj.shape[0]
    tm = min(T, 512)
    tab_blocks = tab.shape[1] // tm
    full = lambda a: pl.BlockSpec(a.shape, lambda i: (0,) * a.ndim)
    of_layer = lambda a: pl.BlockSpec((None,) + a.shape[1:], lambda i: (layer,) + (0,) * (a.ndim - 1))
    in_specs = [pl.BlockSpec((tm, Q_RANK), lambda i: (i, COL_CQ * LANES // Q_RANK)),
                pl.BlockSpec((tm, KV_RANK), lambda i: (i, COL_CKV * LANES // KV_RANK)),
                pl.BlockSpec((tm, LANES), lambda i: (i, COL_GATE)),
                pl.BlockSpec((3, tm, LANES), lambda i: (0, i % tab_blocks, 0)),
                full(gq), full(gkv), of_layer(wqn), of_layer(wqr)] + [of_layer(w) for w in extra_w]
    out_specs = [pl.BlockSpec((tm, KV_RANK), lambda i: (i, 0)), pl.BlockSpec((tm, D_ROPE), lambda i: (i, 0))]
    out_shape = [jax.ShapeDtypeStruct((T, KV_RANK), F32), jax.ShapeDtypeStruct((T, D_ROPE), F32)]
    if sample:
        kern = _mla_prep_sample_kernel
        out_specs.append(pl.BlockSpec((H_MLA, tm, KV_RANK + LANES), lambda i: (0, i, 0)))
        out_shape.append(jax.ShapeDtypeStruct((H_MLA, T, KV_RANK + LANES), F32))
    else:
        kern = _mla_prep_prompt_kernel
        out_specs += [pl.BlockSpec((tm, 2 * D_MLA), lambda i: (i, 0)), pl.BlockSpec((tm, 2 * D_MLA), lambda i: (i, 0)),
                      pl.BlockSpec((tm, D_MLA), lambda i: (i, 0))]
        out_shape += [jax.ShapeDtypeStruct((T, 2 * D_MLA), BF16), jax.ShapeDtypeStruct((T, 2 * D_MLA), BF16),
                      jax.ShapeDtypeStruct((T, D_MLA), BF16)]
    return pl.pallas_call(
        kern, grid=(T // tm,), in_specs=in_specs, out_specs=out_specs, out_shape=out_shape,
        compiler_params=_cparams("parallel"),
        name="mla_prep_sample" if sample else "mla_prep_prompt")(proj, proj, proj, tab, gq, gkv, wqn, wqr, *extra_w)


def _attn_prompt_kernel(q_ref, k_ref, v_ref, z_ref, o_ref, *, tq):
    L = q_ref.shape[0]
    row = lax.broadcasted_iota(jnp.int32, (tq, tq), 0)
    col = lax.broadcasted_iota(jnp.int32, (tq, tq), 1)
    for i in range(L // tq):
        cur = slice(i * tq, (i + 1) * tq)
        q = q_ref[cur, :]
        sd = jnp.where(col <= row, _dot_nt(q, k_ref[cur, :]), -jnp.inf)
        m = jnp.max(sd, axis=1, keepdims=True)
        if i > 0:
            so = _dot_nt(q, k_ref[0:i * tq, :])
            m = jnp.maximum(m, jnp.max(so, axis=1, keepdims=True))
            po = jnp.exp(so - m)
            l = jnp.sum(po, axis=1, keepdims=True)
            acc = _dot(po.astype(BF16), v_ref[0:i * tq, :])
        pd = jnp.exp(sd - m)
        if i > 0:
            l = l + jnp.sum(pd, axis=1, keepdims=True)
            acc = acc + _dot(pd.astype(BF16), v_ref[cur, :])
        else:
            l = jnp.sum(pd, axis=1, keepdims=True)
            acc = _dot(pd.astype(BF16), v_ref[cur, :])
        o_ref[cur, :] = (acc / l * _silu(z_ref[cur, :])).astype(o_ref.dtype)


def _attn_prompt(q, k, v, proj, B, L):
    tq = min(L, 512)
    return pl.pallas_call(
        functools.partial(_attn_prompt_kernel, tq=tq), grid=(B, H_MLA),
        in_specs=[pl.BlockSpec((L, 2 * LANES), lambda b, h: (b, h)),
                  pl.BlockSpec((L, 2 * LANES), lambda b, h: (b, h)),
                  pl.BlockSpec((L, HEAD_DIM), lambda b, h: (b, h)),
                  pl.BlockSpec((L, HEAD_DIM), lambda b, h: (b, COL_AZ + h))],
        out_specs=pl.BlockSpec((L, HEAD_DIM), lambda b, h: (b, h)),
        out_shape=jax.ShapeDtypeStruct((B * L, D_MLA), BF16),
        compiler_params=_cparams("parallel", "parallel"), name="attn_prompt")(q, k, v, proj)


DECODE_SLOTS = 3


def _decode_kernel(pt_ref, qs_ref, cn_ref, kn_ref, ckv_hbm, krt_hbm, o_ref, ckv_buf, krt_buf, sem,
                   *, layer, n_pages, page, n_seq):
    i = pl.program_id(0)

    def page_copies(seq, slot):
        out = []
        for p in range(n_pages):
            pg = pt_ref[seq, p]
            rows = pl.ds(p * page, page)
            out.append(pltpu.make_async_copy(ckv_hbm.at[layer, pg], ckv_buf.at[slot, rows, :], sem.at[0, slot]))
            out.append(pltpu.make_async_copy(krt_hbm.at[layer, pg], krt_buf.at[slot, :, rows], sem.at[1, slot]))
        return out

    def start_fetch(seq, slot):
        for cp in page_copies(seq, slot):
            cp.start()

    def wait_fetch(seq, slot):
        for cp in page_copies(seq, slot):
            cp.wait()

    ahead = DECODE_SLOTS - 1

    @pl.when(i == 0)
    def _():
        for j in range(min(ahead, n_seq)):
            start_fetch(j, j)

    @pl.when(i + ahead < n_seq)
    def _():
        start_fetch(i + ahead, lax.rem(i + ahead, DECODE_SLOTS))

    slot = lax.rem(i, DECODE_SLOTS)
    wait_fetch(i, slot)

    q = qs_ref[0]
    q16 = jnp.concatenate([q, jnp.zeros_like(q)], axis=0).astype(BF16)
    ql = q16[:, 0:KV_RANK]
    qr = q16[:, KV_RANK:KV_RANK + D_ROPE]
    kc = ckv_buf[slot].astype(BF16)
    sc = _dot_nt(ql, kc) + _dot(qr, krt_buf[slot].astype(BF16))
    cn = cn_ref[0].astype(BF16).astype(F32)
    kn = kn_ref[0].astype(BF16).astype(F32)
    s_new = (jnp.sum(ql.astype(F32) * cn, axis=1, keepdims=True)
             + jnp.sum(qr.astype(F32) * kn, axis=1, keepdims=True))
    m = jnp.maximum(jnp.max(sc, axis=1, keepdims=True), s_new)
    p = jnp.exp(sc - m)
    p_new = jnp.exp(s_new - m)
    l = jnp.sum(p, axis=1, keepdims=True) + p_new
    pb = p.astype(BF16)
    half = kc.shape[0] // 2
    acc = (_dot(pb[:, :half], kc[:half]) + _dot(pb[:, half:], kc[half:])
           + p_new.astype(BF16).astype(F32) * cn)
    o_ref[0] = (acc / l)[0:H_MLA, :]


def _decode(page_table, qs, ckv_new, kr_new, cache_ckv, cache_krope_t, layer):
    DB, n_pages = page_table.shape
    page = cache_ckv.shape[2]
    P = n_pages * page
    kern = functools.partial(_decode_kernel, layer=layer, n_pages=n_pages, page=page, n_seq=DB)
    grid_spec = pltpu.PrefetchScalarGridSpec(
        num_scalar_prefetch=1, grid=(DB,),
        in_specs=[pl.BlockSpec((1, H_MLA, KV_RANK + LANES), lambda b, pt: (b, 0, 0)),
                  pl.BlockSpec((1, 1, KV_RANK), lambda b, pt: (b, 0, 0)),
                  pl.BlockSpec((1, 1, D_ROPE), lambda b, pt: (b, 0, 0)),
                  pl.BlockSpec(memory_space=pl.ANY), pl.BlockSpec(memory_space=pl.ANY)],
        out_specs=pl.BlockSpec((1, H_MLA, KV_RANK), lambda b, pt: (b, 0, 0)),
        scratch_shapes=[pltpu.VMEM((DECODE_SLOTS, P, KV_RANK), F32), pltpu.VMEM((DECODE_SLOTS, D_ROPE, P), F32),
                        pltpu.SemaphoreType.DMA((2, DECODE_SLOTS))])
    return pl.pallas_call(
        kern, grid_spec=grid_spec, out_shape=jax.ShapeDtypeStruct((DB, H_MLA, KV_RANK), F32),
        compiler_params=_cparams("arbitrary"), name="mla_decode")(
            page_table, qs, ckv_new.reshape(DB, 1, KV_RANK), kr_new.reshape(DB, 1, D_ROPE), cache_ckv, cache_krope_t)


def _mla_out_kernel(ol_ref, w_ref, z_ref, o_ref):
    o = _dot(ol_ref[0].astype(BF16), w_ref[...])
    o_ref[...] = (o * _silu(z_ref[...])).astype(o_ref.dtype)


def _mla_out(o_lat_h, wuv, layer, proj):
    T = proj.shape[0]
    return pl.pallas_call(
        _mla_out_kernel, grid=(H_MLA,),
        in_specs=[pl.BlockSpec((1, T, KV_RANK), lambda h: (h, 0, 0)),
                  pl.BlockSpec((None, None, KV_RANK, HEAD_DIM), lambda h: (layer, h, 0, 0)),
                  pl.BlockSpec((T, HEAD_DIM), lambda h: (0, COL_AZ + h))],
        out_specs=pl.BlockSpec((T, HEAD_DIM), lambda h: (0, h)),
        out_shape=jax.ShapeDtypeStruct((T, D_MLA), BF16),
        compiler_params=_cparams("parallel"), name="mla_out")(o_lat_h, wuv, proj)


def _out_proj_kernel(or_ref, om_ref, oa_ref, x_ref, w_ref, g_ref, *out_refs, last):
    x = (x_ref[...] + _dot(or_ref[...], w_ref[0:D_RET, :]) + _dot(om_ref[...], w_ref[D_RET:D_RET + D_ML, :])
         + _dot(oa_ref[...], w_ref[D_RET + D_ML:, :]))
    hn = _rms(x, g_ref[...])
    if last:
        out_refs[0][...] = hn
    else:
        out_refs[0][...] = x
        out_refs[1][...] = hn.astype(out_refs[1].dtype)


def _out_proj(o_ret, o_ml, o_mla, x, w, layer, g, last):
    T, D = x.shape
    tm = min(T, 512)
    row = lambda n: pl.BlockSpec((tm, n), lambda i: (i, 0))
    if last:
        out_specs, out_shape = [row(D)], [jax.ShapeDtypeStruct((T, D), F32)]
    else:
        out_specs = [row(D), row(D)]
        out_shape = [jax.ShapeDtypeStruct((T, D), F32), jax.ShapeDtypeStruct((T, D), BF16)]
    return pl.pallas_call(
        functools.partial(_out_proj_kernel, last=last), grid=(T // tm,),
        in_specs=[row(D_RET), row(D_ML), row(D_MLA), row(D),
                  pl.BlockSpec((None,) + w.shape[1:], lambda i: (layer, 0, 0)),
                  pl.BlockSpec((1, D), lambda i: (0, 0))],
        out_specs=out_specs, out_shape=out_shape,
        compiler_params=_cparams("parallel"), name="out_proj")(o_ret, o_ml, o_mla, x, w, g.reshape(1, D))


def _rope_tables(pos):
    pos = pos.astype(F32)[:, None]

    def cs(half):
        inv = ROPE_BASE ** (-jnp.arange(half, dtype=F32) / half)
        ang = pos * inv[None, :]
        return jnp.cos(ang), jnp.sin(ang)

    c64, s64 = cs(HEAD_DIM // 2)
    tab128 = jnp.stack([jnp.concatenate([c64, c64], -1), jnp.concatenate([-s64, s64], -1)])
    c32, s32 = cs(D_ROPE // 2)
    z32 = jnp.zeros_like(c32)
    tab64 = jnp.stack([jnp.concatenate([c32, c32, z32, z32], -1),
                       jnp.concatenate([-s32, z32, z32, z32], -1),
                       jnp.concatenate([z32, s32, z32, z32], -1)])
    return tab128, tab64


def _layout_w_tail(wt):
    o = np.cumsum((0,) + IN_WIDTHS)
    assert o[9] == N_MAIN
    seg = lambda a, b: wt[:, o[a]:o[b], :]
    pad = jnp.zeros((wt.shape[0], N_TAIL - (D_IN - N_MAIN), wt.shape[2]), wt.dtype)
    return jnp.concatenate([seg(11, 12), seg(12, 13), seg(14, 15), seg(13, 14), seg(9, 11), pad], axis=1).astype(BF16)


def kernel(x_prompt, x_sample, state_ret, state_mlstm_c, state_mlstm_n, state_mlstm_m, cache_ckv, cache_krope,
           page_table, g_norm, w_in, b_ig, b_fg, g_ret, g_ml, g_q, g_kv, w_uq, w_uk, w_uv, w_out, g_final):
    B, L, D = x_prompt.shape
    DB, Ls, _ = x_sample.shape
    assert Ls == 1, "the sample group is one new token per sequence"
    depth = w_in.shape[0]
    past_len = page_table.shape[1] * cache_ckv.shape[2]

    tab128_p, tab64_p = _rope_tables(jnp.arange(L))
    tab128_s, tab64_s = _rope_tables(past_len + jnp.arange(Ls))
    tab128_s = jnp.broadcast_to(tab128_s, (2, DB, HEAD_DIM))
    tab64_s = jnp.broadcast_to(tab64_s, (3, DB, LANES))
    w_in_t = jnp.swapaxes(w_in, 1, 2)
    w_tail_b = _layout_w_tail(w_in_t)
    w_out_b = w_out.astype(BF16)
    w_uq_t = jnp.transpose(w_uq, (0, 2, 3, 1))
    wqn = w_uq_t[:, :, :D_NOPE, :].reshape(depth, D_MLA, Q_RANK).astype(BF16)
    wqr = jnp.concatenate([w_uq_t[:, :, D_NOPE:, :], jnp.zeros((depth, H_MLA, LANES - D_ROPE, Q_RANK), w_uq.dtype)],
                          2).reshape(depth, D_MLA, Q_RANK).astype(BF16)
    wk = w_uk.reshape(depth, KV_RANK, D_MLA).astype(BF16)
    wv = w_uv.reshape(depth, KV_RANK, D_MLA).astype(BF16)
    wukt = jnp.transpose(w_uk, (0, 2, 3, 1)).astype(BF16)
    wuv_h = jnp.transpose(w_uv, (0, 2, 1, 3)).astype(BF16)
    zeros_gate = jnp.zeros((depth, LANES - D_ROPE - 2 * H_ML), F32)
    gate_bias = jnp.concatenate([jnp.zeros((depth, D_ROPE), F32), b_ig, b_fg, zeros_gate], -1)

    cache_krope_t = jnp.swapaxes(cache_krope, 2, 3)

    xp = x_prompt.reshape(B * L, D)
    xs = x_sample.reshape(DB, D)
    hp = _rmsnorm(xp, g_norm[0], BF16)
    hs = _rmsnorm(xs, g_norm[0], BF16)
    p_out = [[] for _ in range(6)]
    s_out = [[] for _ in range(4)]
    s_ret_all = s_c_all = None
    for l in range(depth):
        last = l == depth - 1
        g_next = g_final if last else g_norm[l + 1]
        gq, gkv = g_q[l].reshape(1, Q_RANK), g_kv[l].reshape(1, KV_RANK)
        gr, gm = g_ret[l].reshape(1, D_RET), g_ml[l].reshape(1, D_ML)
        gb = gate_bias[l].reshape(1, LANES)

        proj = _in_proj(hp, w_in_t, l, N_MAIN)
        tail = _in_proj(hp, w_tail_b, l, N_TAIL)
        o_ret, st_ret = _ret_prompt(proj, tab128_p, gr, B, L)
        o_ml, st_c, st_n, st_m = _mlstm_prompt(proj, tail, gb, gm, B, L)
        ckvn, krope, q, k, v = _mla_prep(tail, tab64_p, gq, gkv, wqn, wqr, (wk, wv), l, sample=False)
        o_mla = _attn_prompt(q, k, v, tail, B, L)
        res = _out_proj(o_ret, o_ml, o_mla, xp, w_out_b, l, g_next, last)
        if last:
            y_prompt = res[0]
        else:
            xp, hp = res
        for j, t in enumerate((st_ret, st_c, st_n[:, :, 0, :], st_m[:, :, 0, 0],
                               ckvn.reshape(B, L, KV_RANK), krope.reshape(B, L, D_ROPE))):
            p_out[j].append(t)

        proj_s = _in_proj(hs, w_in_t, l, N_MAIN)
        tail_s = _in_proj(hs, w_tail_b, l, N_TAIL)
        o_ret, s_ret_all = _ret_step(proj_s, tab128_s, gr, state_ret, s_ret_all, l)
        o_ml, s_c_all, st_n, st_m = _mlstm_step(proj_s, tail_s, gb, gm, state_mlstm_c, state_mlstm_n,
                                                state_mlstm_m, s_c_all, l)
        ckvn, krope, qs = _mla_prep(tail_s, tab64_s, gq, gkv, wqn, wqr, (wukt,), l, sample=True)
        o_lat = _decode(page_table, jnp.transpose(qs, (1, 0, 2)), ckvn, krope, cache_ckv, cache_krope_t, l)
        o_mla = _mla_out(jnp.transpose(o_lat, (1, 0, 2)), wuv_h, l, tail_s)
        res = _out_proj(o_ret, o_ml, o_mla, xs, w_out_b, l, g_next, last)
        if last:
            y_sample = res[0]
        else:
            xs, hs = res
        for j, t in enumerate((st_n, st_m.reshape(DB, H_ML, HEAD_DIM)[:, :, 0],
                               ckvn.reshape(DB, Ls, KV_RANK), krope.reshape(DB, Ls, D_ROPE))):
            s_out[j].append(t)

    return (y_prompt.reshape(B, L, D), y_sample.reshape(DB, Ls, D),
            *[jnp.stack(t) for t in p_out], s_ret_all, s_c_all, *[jnp.stack(t) for t in s_out])
```

```python
import functools

import numpy as np
import jax
import jax.numpy as jnp
from jax import lax
from jax.experimental import pallas as pl
from jax.experimental.pallas import tpu as pltpu

F32 = jnp.float32
BF16 = jnp.bfloat16

HEAD_DIM = 128
H_RET = 4
H_ML = 4
H_MLA = 8
D_RET = H_RET * HEAD_DIM
D_ML = H_ML * HEAD_DIM
D_MLA = H_MLA * HEAD_DIM
Q_RANK = 512
KV_RANK = 256
D_NOPE = 128
D_ROPE = 64
ROPE_BASE = 10000.0
MIX_CHUNK = 128
NORM_EPS = 1e-6
MLSTM_EPS = 1e-6
ATT_SCALE = (D_NOPE + D_ROPE) ** -0.5
QK_SCALE = HEAD_DIM ** -0.5
IN_WIDTHS = (D_RET, D_RET, D_RET, D_RET, D_ML, D_ML, D_ML, D_ML, D_ML, H_ML, H_ML,
             Q_RANK, KV_RANK, D_ROPE, D_MLA)
D_IN = sum(IN_WIDTHS)

LANES = 128
VMEM_LIMIT_BYTES = 56 * 1024 * 1024

COL_RET = 0
COL_ML = 16
COL_CQ = 36
COL_CKV = 40
COL_AZ = 42
COL_GATE = 50
GATE_I_LANE = D_ROPE
GATE_F_LANE = D_ROPE + H_ML
N_IN_PAD = 52 * LANES
RET_LOG_G = [float(np.log1p(-np.exp2(np.float32(-5.0 - h)))) for h in range(H_RET)]


def _cparams(*sem):
    return pltpu.CompilerParams(dimension_semantics=sem, vmem_limit_bytes=VMEM_LIMIT_BYTES)


def _dot(a, b):
    return jnp.dot(a, b, preferred_element_type=F32)


def _dot_nt(a, b):
    return lax.dot_general(a, b, (((1,), (1,)), ((), ())), preferred_element_type=F32)


def _silu(x):
    return x * jax.nn.sigmoid(x)


def _rms(x, g):
    return x * lax.rsqrt(jnp.mean(x * x, axis=-1, keepdims=True) + NORM_EPS) * g


def _head_norm(o, g):
    mu = jnp.mean(o, axis=-1, keepdims=True)
    d = o - mu
    var = jnp.mean(d * d, axis=-1, keepdims=True)
    return d * lax.rsqrt(var + NORM_EPS) * g


def _rope128(x, cos, sin_signed):
    return x * cos + pltpu.roll(x, HEAD_DIM // 2, axis=1) * sin_signed


def _rope128b(x, cos, sin_signed):
    return x * cos + pltpu.roll(x, HEAD_DIM // 2, axis=2) * sin_signed


def _rope64(x, tab):
    q = D_ROPE // 2
    return x * tab[0] + pltpu.roll(x, LANES - q, axis=1) * tab[1] + pltpu.roll(x, q, axis=1) * tab[2]


def _rmsnorm_kernel(x_ref, g_ref, o_ref):
    o_ref[...] = _rms(x_ref[...], g_ref[...]).astype(o_ref.dtype)


def _rmsnorm(x, g, dtype):
    T, D = x.shape
    tm = min(T, 512)
    return pl.pallas_call(
        _rmsnorm_kernel, grid=(T // tm,),
        in_specs=[pl.BlockSpec((tm, D), lambda i: (i, 0)), pl.BlockSpec((1, D), lambda i: (0, 0))],
        out_specs=pl.BlockSpec((tm, D), lambda i: (i, 0)),
        out_shape=jax.ShapeDtypeStruct((T, D), dtype),
        compiler_params=_cparams("parallel"), name="rmsnorm")(x, g.reshape(1, D))


def _matmul_nt_kernel(h_ref, wt_ref, o_ref):
    o_ref[...] = _dot_nt(h_ref[...], wt_ref[...])


def _in_proj(hn, wt, layer, N):
    T, D = hn.shape
    tm = min(T, 2048)
    tn = 4 * LANES
    return pl.pallas_call(
        _matmul_nt_kernel, grid=(T // tm, N // tn),
        in_specs=[pl.BlockSpec((tm, D), lambda i, j: (i, 0)),
                  pl.BlockSpec((None, tn, D), lambda i, j: (layer, j, 0))],
        out_specs=pl.BlockSpec((tm, tn), lambda i, j: (i, j)),
        out_shape=jax.ShapeDtypeStruct((T, N), F32),
        compiler_params=_cparams("parallel", "arbitrary"), name="in_proj")(hn, wt)


def _ret_prompt_kernel(q_ref, k_ref, v_ref, z_ref, cs_ref, g_ref, o_ref, s_ref, S_scr):
    ci = pl.program_id(0)
    B, c = q_ref.shape[0], q_ref.shape[1]

    @pl.when(ci == 0)
    def _():
        S_scr[...] = jnp.zeros_like(S_scr)

    cos = cs_ref[0]
    sin = cs_ref[1]
    row = lax.broadcasted_iota(jnp.int32, (c, c), 0)
    col = lax.broadcasted_iota(jnp.int32, (c, c), 1)
    rel = (row - col).astype(F32)
    idx = lax.broadcasted_iota(jnp.int32, (c, 1), 0).astype(F32)
    for h in range(H_RET):
        sl = slice(h * HEAD_DIM, (h + 1) * HEAD_DIM)
        lg = RET_LOG_G[h]
        decay = jnp.where(rel >= 0, jnp.exp(lg * jnp.maximum(rel, 0.0)), 0.0)
        q_dec = jnp.exp(lg * (idx + 1.0))
        k_dec = jnp.exp(lg * (c - 1.0 - idx))
        bmm = lambda e, x, y: jnp.einsum(e, x, y, preferred_element_type=F32)
        q = _rope128b(q_ref[:, :, sl], cos, sin)
        k = _rope128b(k_ref[:, :, sl], cos, sin) * QK_SCALE
        vb = v_ref[:, :, sl].astype(BF16)
        qb = q.astype(BF16)
        S = S_scr[:, h]
        sc = bmm('bik,bjk->bij', qb, k.astype(BF16)) * decay
        o = bmm('bij,bjv->biv', sc.astype(BF16), vb) + bmm('bik,bkv->biv', qb, S.astype(BF16)) * q_dec
        kdt = jnp.swapaxes(k * k_dec, 1, 2).astype(BF16)
        S_scr[:, h] = S * float(np.exp(lg * c)) + bmm('bkj,bjv->bkv', kdt, vb)
        y = _head_norm(o, g_ref[:, sl]) * _silu(z_ref[:, :, sl])
        o_ref[:, :, sl] = y.astype(o_ref.dtype)

    @pl.when(ci == pl.num_programs(0) - 1)
    def _():
        s_ref[...] = S_scr[...]


def _ret_prompt(proj, cs, g, B, L):
    c = MIX_CHUNK if L % MIX_CHUNK == 0 else L
    wb = D_RET // LANES
    col = lambda j: (lambda i: (0, i, COL_RET // wb + j))
    proj3 = proj.reshape(B, L, proj.shape[1])
    o, st = pl.pallas_call(
        _ret_prompt_kernel, grid=(L // c,),
        in_specs=[pl.BlockSpec((B, c, D_RET), col(0)), pl.BlockSpec((B, c, D_RET), col(1)),
                  pl.BlockSpec((B, c, D_RET), col(2)), pl.BlockSpec((B, c, D_RET), col(3)),
                  pl.BlockSpec((2, c, HEAD_DIM), lambda i: (0, i, 0)),
                  pl.BlockSpec((1, D_RET), lambda i: (0, 0))],
        out_specs=[pl.BlockSpec((B, c, D_RET), lambda i: (0, i, 0)),
                   pl.BlockSpec((B, H_RET, HEAD_DIM, HEAD_DIM), lambda i: (0, 0, 0, 0))],
        out_shape=[jax.ShapeDtypeStruct((B, L, D_RET), BF16),
                   jax.ShapeDtypeStruct((B, H_RET, HEAD_DIM, HEAD_DIM), F32)],
        scratch_shapes=[pltpu.VMEM((B, H_RET, HEAD_DIM, HEAD_DIM), F32)],
        compiler_params=_cparams("arbitrary"), name="ret_prompt")(proj3, proj3, proj3, proj3, cs, g)
    return o.reshape(B * L, D_RET), st


def _log_sigmoid(x):
    return -(jnp.maximum(-x, 0.0) + jnp.log1p(jnp.exp(-jnp.abs(x))))


def _mlstm_prompt_kernel(q_ref, k_ref, v_ref, og_ref, z_ref, gate_ref, bias_ref, g_ref,
                         o_ref, c_ref, n_ref, m_ref, C_scr, M_scr):
    ci = pl.program_id(0)
    B, c = q_ref.shape[0], q_ref.shape[1]

    @pl.when(ci == 0)
    def _():
        C_scr[...] = jnp.zeros_like(C_scr)
        M_scr[...] = jnp.zeros_like(M_scr)

    row = lax.broadcasted_iota(jnp.int32, (c, c), 0)
    col = lax.broadcasted_iota(jnp.int32, (c, c), 1)
    causal = col <= row
    eye = col == row
    ones = jnp.ones((B, c, HEAD_DIM), BF16)
    gates = gate_ref[...] + bias_ref[...]
    bmm = lambda e, x, y: jnp.einsum(e, x, y, preferred_element_type=F32)
    for h in range(H_ML):
        sl = slice(h * HEAD_DIM, (h + 1) * HEAD_DIM)
        q = q_ref[:, :, sl]
        k = k_ref[:, :, sl] * QK_SCALE
        vb1 = jnp.concatenate([v_ref[:, :, sl].astype(BF16), ones], axis=2)
        i_col = gates[:, :, GATE_I_LANE + h:GATE_I_LANE + h + 1]
        lf_col = _log_sigmoid(gates[:, :, GATE_F_LANE + h:GATE_F_LANE + h + 1])
        lf_row = jnp.sum(jnp.where(eye, lf_col, 0.0), axis=1, keepdims=True)
        i_row = jnp.sum(jnp.where(eye, i_col, 0.0), axis=1, keepdims=True)
        b_col = jnp.sum(jnp.where(causal, lf_row, 0.0), axis=2, keepdims=True)
        b_row = jnp.sum(jnp.where(row <= col, lf_col, 0.0), axis=1, keepdims=True)
        m_prev = M_scr[:, h][:, :, 0:1]
        log_d = jnp.where(causal, b_col - b_row + i_row, -jnp.inf)
        inter = b_col + m_prev
        m_t = jnp.maximum(inter, jnp.max(log_d, axis=2, keepdims=True))
        w_intra = jnp.exp(log_d - m_t)
        w_inter = jnp.exp(inter - m_t)
        qb = q.astype(BF16)
        C1 = C_scr[:, h]
        s = bmm('bik,bjk->bij', qb, k.astype(BF16)) * w_intra
        a = bmm('bij,bjv->biv', s.astype(BF16), vb1) + w_inter * bmm('bik,bkv->biv', qb, C1.astype(BF16))
        hb = a[:, :, :HEAD_DIM] / (jnp.maximum(jnp.abs(a[:, :, HEAD_DIM:]), jnp.exp(-m_t)) + MLSTM_EPS)
        m_new = m_t[:, c - 1:c, :]
        b_last = b_col[:, c - 1:c, :]
        w_k = jnp.exp(b_last - b_col + i_col - m_new)
        dec = jnp.exp(b_last + m_prev - m_new)
        kwt = jnp.swapaxes(k * w_k, 1, 2).astype(BF16)
        C_scr[:, h] = dec * C1 + bmm('bkj,bjv->bkv', kwt, vb1)
        M_scr[:, h] = jnp.broadcast_to(m_new, (B, 1, HEAD_DIM))
        y = _head_norm(hb * jax.nn.sigmoid(og_ref[:, :, sl]), g_ref[:, sl]) * _silu(z_ref[:, :, sl])
        o_ref[:, :, sl] = y.astype(o_ref.dtype)

    @pl.when(ci == pl.num_programs(0) - 1)
    def _():
        m_ref[...] = M_scr[...]
        for b in range(B):
            for h in range(H_ML):
                C1 = C_scr[b, h]
                c_ref[b, h] = C1[:, :HEAD_DIM]
                n_ref[b, h] = C1[:, HEAD_DIM:].T[0:1, :]


def _mlstm_prompt(proj, proj_tail, gate_bias, g, B, L):
    c = MIX_CHUNK if L % MIX_CHUNK == 0 else L
    wb = D_ML // LANES
    col = lambda j: (lambda i: (0, i, COL_ML // wb + j))
    proj3 = proj.reshape(B, L, proj.shape[1])
    tail3 = proj_tail.reshape(B, L, proj_tail.shape[1])
    vec = jax.ShapeDtypeStruct((B, H_ML, 1, HEAD_DIM), F32)
    vec_spec = pl.BlockSpec((B, H_ML, 1, HEAD_DIM), lambda i: (0, 0, 0, 0))
    o, st_c, st_n, st_m = pl.pallas_call(
        _mlstm_prompt_kernel, grid=(L // c,),
        in_specs=[pl.BlockSpec((B, c, D_ML), col(0)), pl.BlockSpec((B, c, D_ML), col(1)),
                  pl.BlockSpec((B, c, D_ML), col(2)), pl.BlockSpec((B, c, D_ML), col(3)),
                  pl.BlockSpec((B, c, D_ML), col(4)),
                  pl.BlockSpec((B, c, LANES), lambda i: (0, i, COL_GATE)),
                  pl.BlockSpec((1, LANES), lambda i: (0, 0)),
                  pl.BlockSpec((1, D_ML), lambda i: (0, 0))],
        out_specs=[pl.BlockSpec((B, c, D_ML), lambda i: (0, i, 0)),
                   pl.BlockSpec((B, H_ML, HEAD_DIM, HEAD_DIM), lambda i: (0, 0, 0, 0)),
                   vec_spec, vec_spec],
        out_shape=[jax.ShapeDtypeStruct((B, L, D_ML), BF16),
                   jax.ShapeDtypeStruct((B, H_ML, HEAD_DIM, HEAD_DIM), F32), vec, vec],
        scratch_shapes=[pltpu.VMEM((B, H_ML, HEAD_DIM, 2 * HEAD_DIM), F32),
                        pltpu.VMEM((B, H_ML, 1, HEAD_DIM), F32)],
        compiler_params=_cparams("arbitrary"),
        name="mlstm_prompt")(proj3, proj3, proj3, proj3, proj3, tail3, gate_bias, g)
    return o.reshape(B * L, D_ML), st_c, st_n, st_m


STEP_TOKENS = 16


def _outer_t(kb, vm):
    return lax.dot_general(kb, vm, (((0,), (0,)), ((), ())), preferred_element_type=F32)


def _ret_step_kernel(q_ref, k_ref, v_ref, z_ref, cs_ref, g_ref, s_ref, *rest):
    o_ref, so_ref = rest[-2:]
    tb = q_ref.shape[0]
    cos = cs_ref[0]
    sin = cs_ref[1]
    tok = lax.broadcasted_iota(jnp.int32, (tb, HEAD_DIM), 0)
    for h in range(H_RET):
        sl = slice(h * HEAD_DIM, (h + 1) * HEAD_DIM)
        gamma = float(np.exp(RET_LOG_G[h]))
        q = _rope128(q_ref[:, sl], cos, sin)
        k = _rope128(k_ref[:, sl], cos, sin) * QK_SCALE
        v = v_ref[:, sl]
        qb = q.astype(BF16)
        kb = k.astype(BF16)
        ro = jnp.zeros((tb, HEAD_DIM), F32)
        for t in range(tb):
            S = s_ref[0, t, h]
            ro = jnp.where(tok == t, _dot(qb, S.astype(BF16)), ro)
            vm = jnp.where(tok == t, v, 0.0).astype(BF16)
            so_ref[0, t, h] = gamma * S + _outer_t(kb, vm)
        o = jnp.sum(q * k, axis=1, keepdims=True) * v + gamma * ro
        y = _head_norm(o, g_ref[:, sl]) * _silu(z_ref[:, sl])
        o_ref[:, sl] = y.astype(o_ref.dtype)


def _stacked_state_specs(prev, depth, T, tb, H, layer):
    shape = jax.ShapeDtypeStruct((depth, T, H, HEAD_DIM, HEAD_DIM), F32)
    spec = pl.BlockSpec((1, tb, H, HEAD_DIM, HEAD_DIM), lambda i: (layer, i, 0, 0, 0))
    if prev is None:
        prev = jnp.zeros(shape.shape, shape.dtype)
    return shape, spec, [pl.BlockSpec(memory_space=pl.ANY)], [prev]


def _ret_step(proj, cs, g, state, prev, layer):
    T = proj.shape[0]
    depth = state.shape[0]
    tb = min(T, STEP_TOKENS)
    wb = D_RET // LANES
    col = lambda j: (lambda i: (i, COL_RET // wb + j))
    st_shape, st_spec, extra_specs, extra_args = _stacked_state_specs(prev, depth, T, tb, H_RET, layer)
    n_in = 7
    return pl.pallas_call(
        _ret_step_kernel, grid=(T // tb,),
        in_specs=[pl.BlockSpec((tb, D_RET), col(0)), pl.BlockSpec((tb, D_RET), col(1)),
                  pl.BlockSpec((tb, D_RET), col(2)), pl.BlockSpec((tb, D_RET), col(3)),
                  pl.BlockSpec((2, tb, HEAD_DIM), lambda i: (0, i, 0)),
                  pl.BlockSpec((1, D_RET), lambda i: (0, 0)),
                  pl.BlockSpec((1, tb, H_RET, HEAD_DIM, HEAD_DIM), lambda i: (layer, i, 0, 0, 0))] + extra_specs,
        out_specs=[pl.BlockSpec((tb, D_RET), lambda i: (i, 0)), st_spec],
        out_shape=[jax.ShapeDtypeStruct((T, D_RET), BF16), st_shape],
        input_output_aliases={n_in: 1},
        compiler_params=_cparams("parallel"), name="ret_step")(proj, proj, proj, proj, cs, g, state, *extra_args)


def _mlstm_step_kernel(q_ref, k_ref, v_ref, og_ref, z_ref, gate_ref, bias_ref, g_ref, c_ref, n_ref, m_ref,
                       *rest):
    o_ref, co_ref, no_ref, mo_ref = rest[-4:]
    tb = q_ref.shape[0]
    gates = gate_ref[...] + bias_ref[...]
    m_all = m_ref[0]
    tok = lax.broadcasted_iota(jnp.int32, (tb, HEAD_DIM), 0)
    for h in range(H_ML):
        sl = slice(h * HEAD_DIM, (h + 1) * HEAD_DIM)
        q = q_ref[:, sl]
        k = k_ref[:, sl] * QK_SCALE
        v = v_ref[:, sl]
        ib = gates[:, GATE_I_LANE + h:GATE_I_LANE + h + 1]
        lf = _log_sigmoid(gates[:, GATE_F_LANE + h:GATE_F_LANE + h + 1])
        inter = lf + m_all[:, h:h + 1]
        m_t = jnp.maximum(inter, ib)
        w_intra = jnp.exp(ib - m_t)
        w_inter = jnp.exp(inter - m_t)
        nv = n_ref[0, :, h, :]
        qb = q.astype(BF16)
        kb = k.astype(BF16)
        vw = v * w_intra
        ro = jnp.zeros((tb, HEAD_DIM), F32)
        for t in range(tb):
            Cm = c_ref[0, t, h]
            ro = jnp.where(tok == t, _dot(qb, Cm.astype(BF16)), ro)
            vm = jnp.where(tok == t, vw, 0.0).astype(BF16)
            co_ref[0, t, h] = w_inter[t:t + 1, :] * Cm + _outer_t(kb, vm)
        s = jnp.sum(q * k, axis=1, keepdims=True) * w_intra
        num = s * v + w_inter * ro
        qn = s + w_inter * jnp.sum(q * nv, axis=1, keepdims=True)
        hb = num / (jnp.maximum(jnp.abs(qn), jnp.exp(-m_t)) + MLSTM_EPS)
        no_ref[:, h, :] = w_inter * nv + w_intra * k
        mo_ref[:, sl] = jnp.broadcast_to(m_t, (tb, HEAD_DIM))
        y = _head_norm(hb * jax.nn.sigmoid(og_ref[:, sl]), g_ref[:, sl]) * _silu(z_ref[:, sl])
        o_ref[:, sl] = y.astype(o_ref.dtype)


def _mlstm_step(proj, proj_tail, gate_bias, g, c_state, n_state, m_state, prev, layer):
    T = proj.shape[0]
    depth = c_state.shape[0]
    tb = min(T, STEP_TOKENS)
    wb = D_ML // LANES
    col = lambda j: (lambda i: (i, COL_ML // wb + j))
    st_shape, st_spec, extra_specs, extra_args = _stacked_state_specs(prev, depth, T, tb, H_ML, layer)
    n_in = 11
    return pl.pallas_call(
        _mlstm_step_kernel, grid=(T // tb,),
        in_specs=[pl.BlockSpec((tb, D_ML), col(0)), pl.BlockSpec((tb, D_ML), col(1)),
                  pl.BlockSpec((tb, D_ML), col(2)), pl.BlockSpec((tb, D_ML), col(3)),
                  pl.BlockSpec((tb, D_ML), col(4)),
                  pl.BlockSpec((tb, LANES), lambda i: (i, COL_GATE)),
                  pl.BlockSpec((1, LANES), lambda i: (0, 0)),
                  pl.BlockSpec((1, D_ML), lambda i: (0, 0)),
                  pl.BlockSpec((1, tb, H_ML, HEAD_DIM, HEAD_DIM), lambda i: (layer, i, 0, 0, 0)),
                  pl.BlockSpec((1, tb, H_ML, HEAD_DIM), lambda i: (layer, i, 0, 0)),
                  pl.BlockSpec((1, tb, H_ML), lambda i: (layer, i, 0))] + extra_specs,
        out_specs=[pl.BlockSpec((tb, D_ML), lambda i: (i, 0)), st_spec,
                   pl.BlockSpec((tb, H_ML, HEAD_DIM), lambda i: (i, 0, 0)),
                   pl.BlockSpec((tb, D_ML), lambda i: (i, 0))],
        out_shape=[jax.ShapeDtypeStruct((T, D_ML), BF16), st_shape,
                   jax.ShapeDtypeStruct((T, H_ML, HEAD_DIM), F32),
                   jax.ShapeDtypeStruct((T, D_ML), F32)],
        input_output_aliases={n_in: 1},
        compiler_params=_cparams("parallel"),
        name="mlstm_step")(proj, proj, proj, proj, proj, proj_tail, gate_bias, g, c_state, n_state, m_state,
                           *extra_args)


def _mla_common(cq_ref, ckv_ref, gate_ref, tab_ref, gq_ref, gkv_ref, wqn_ref, wqr_ref, ckvn_ref, kr_ref):
    tab = tab_ref[...]
    cqn = _rms(cq_ref[...], gq_ref[...]).astype(BF16)
    ckvn = _rms(ckv_ref[...], gkv_ref[...])
    ckvn_ref[...] = ckvn
    k_rope = _rope64(gate_ref[...], tab)
    kr_ref[...] = k_rope[:, :D_ROPE]
    q_nope = _dot_nt(cqn, wqn_ref[...])
    q_rope = _dot_nt(cqn, wqr_ref[...])
    return tab, ckvn, k_rope, q_nope, q_rope


def _mla_prep_prompt_kernel(cq_ref, ckv_ref, gate_ref, tab_ref, gq_ref, gkv_ref, wqn_ref, wqr_ref, wk_ref, wv_ref,
                            ckvn_ref, kr_ref, q_ref, k_ref, v_ref):
    tab, ckvn, k_rope, q_nope, q_rope = _mla_common(cq_ref, ckv_ref, gate_ref, tab_ref, gq_ref, gkv_ref,
                                                    wqn_ref, wqr_ref, ckvn_ref, kr_ref)
    cb = ckvn.astype(BF16)
    k_nope = _dot(cb, wk_ref[...])
    v_ref[...] = _dot(cb, wv_ref[...]).astype(v_ref.dtype)
    krb = k_rope.astype(k_ref.dtype)
    for h in range(H_MLA):
        sl = slice(h * HEAD_DIM, (h + 1) * HEAD_DIM)
        lo = slice(2 * h * LANES, (2 * h + 1) * LANES)
        hi = slice((2 * h + 1) * LANES, (2 * h + 2) * LANES)
        q_ref[:, lo] = (q_nope[:, sl] * ATT_SCALE).astype(q_ref.dtype)
        q_ref[:, hi] = (_rope64(q_rope[:, sl], tab) * ATT_SCALE).astype(q_ref.dtype)
        k_ref[:, lo] = k_nope[:, sl].astype(k_ref.dtype)
        k_ref[:, hi] = krb


def _mla_prep_sample_kernel(cq_ref, ckv_ref, gate_ref, tab_ref, gq_ref, gkv_ref, wqn_ref, wqr_ref, wukt_ref,
                            ckvn_ref, kr_ref, qs_ref):
    tab, ckvn, k_rope, q_nope, q_rope = _mla_common(cq_ref, ckv_ref, gate_ref, tab_ref, gq_ref, gkv_ref,
                                                    wqn_ref, wqr_ref, ckvn_ref, kr_ref)
    for h in range(H_MLA):
        sl = slice(h * HEAD_DIM, (h + 1) * HEAD_DIM)
        q_lat = _dot(q_nope[:, sl].astype(BF16), wukt_ref[h])
        qs_ref[h, :, 0:KV_RANK] = q_lat * ATT_SCALE
        qs_ref[h, :, KV_RANK:KV_RANK + LANES] = _rope64(q_rope[:, sl], tab) * ATT_SCALE


def _mla_prep(proj, tab, gq, gkv, wqn, wqr, extra_w, layer, sample):
    T = proj.shape[0]
    tm = min(T, 512)
    tab_blocks = tab.shape[1] // tm
    full = lambda a: pl.BlockSpec(a.shape, lambda i: (0,) * a.ndim)
    of_layer = lambda a: pl.BlockSpec((None,) + a.shape[1:], lambda i: (layer,) + (0,) * (a.ndim - 1))
    in_specs = [pl.BlockSpec((tm, Q_RANK), lambda i: (i, COL_CQ * LANES // Q_RANK)),
                pl.BlockSpec((tm, KV_RANK), lambda i: (i, COL_CKV * LANES // KV_RANK)),
                pl.BlockSpec((tm, LANES), lambda i: (i, COL_GATE)),
                pl.BlockSpec((3, tm, LANES), lambda i: (0, i % tab_blocks, 0)),
                full(gq), full(gkv), of_layer(wqn), of_layer(wqr)] + [of_layer(w) for w in extra_w]
    out_specs = [pl.BlockSpec((tm, KV_RANK), lambda i: (i, 0)), pl.BlockSpec((tm, D_ROPE), lambda i: (i, 0))]
    out_shape = [jax.ShapeDtypeStruct((T, KV_RANK), F32), jax.ShapeDtypeStruct((T, D_ROPE), F32)]
    if sample:
        kern = _mla_prep_sample_kernel
        out_specs.append(pl.BlockSpec((H_MLA, tm, KV_RANK + LANES), lambda i: (0, i, 0)))
        out_shape.append(jax.ShapeDtypeStruct((H_MLA, T, KV_RANK + LANES), F32))
    else:
        kern = _mla_prep_prompt_kernel
        out_specs += [pl.BlockSpec((tm, 2 * D_MLA), lambda i: (i, 0)), pl.BlockSpec((tm, 2 * D_MLA), lambda i: (i, 0)),
                      pl.BlockSpec((tm, D_MLA), lambda i: (i, 0))]
        out_shape += [jax.ShapeDtypeStruct((T, 2 * D_MLA), BF16), jax.ShapeDtypeStruct((T, 2 * D_MLA), BF16),
                      jax.ShapeDtypeStruct((T, D_MLA), BF16)]
    return pl.pallas_call(
        kern, grid=(T // tm,), in_specs=in_specs, out_specs=out_specs, out_shape=out_shape,
        compiler_params=_cparams("parallel"),
        name="mla_prep_sample" if sample else "mla_prep_prompt")(proj, proj, proj, tab, gq, gkv, wqn, wqr, *extra_w)


def _attn_prompt_kernel(q_ref, k_ref, v_ref, z_ref, o_ref, *, tq):
    L = q_ref.shape[0]
    n = L // tq
    row = lax.broadcasted_iota(jnp.int32, (tq, tq), 0)
    col = lax.broadcasted_iota(jnp.int32, (tq, tq), 1)

    def scores(i):
        cur = slice(i * tq, (i + 1) * tq)
        q = q_ref[cur, :]
        sd = _dot_nt(q, k_ref[cur, :])
        so = _dot_nt(q, k_ref[0:i * tq, :]) if i > 0 else None
        return sd, so

    def values(i, pd, po, l):
        cur = slice(i * tq, (i + 1) * tq)
        acc = _dot(pd, v_ref[cur, :])
        if i > 0:
            acc = acc + _dot(po, v_ref[0:i * tq, :])
        o_ref[cur, :] = (acc / l * _silu(z_ref[cur, :])).astype(o_ref.dtype)

    nxt = scores(0)
    prev = None
    for i in range(n):
        sd, so = nxt
        if i + 1 < n:
            nxt = scores(i + 1)
        if prev is not None:
            values(i - 1, *prev)
        sd = jnp.where(col <= row, sd, -jnp.inf)
        m = jnp.max(sd, axis=1, keepdims=True)
        if i > 0:
            m = jnp.maximum(m, jnp.max(so, axis=1, keepdims=True))
        pd = jnp.exp(sd - m)
        l = jnp.sum(pd, axis=1, keepdims=True)
        po = None
        if i > 0:
            po = jnp.exp(so - m)
            l = l + jnp.sum(po, axis=1, keepdims=True)
            po = po.astype(BF16)
        prev = (pd.astype(BF16), po, l)
    values(n - 1, *prev)


def _attn_prompt(q, k, v, proj, B, L):
    tq = min(L, 256)
    return pl.pallas_call(
        functools.partial(_attn_prompt_kernel, tq=tq), grid=(B, H_MLA),
        in_specs=[pl.BlockSpec((L, 2 * LANES), lambda b, h: (b, h)),
                  pl.BlockSpec((L, 2 * LANES), lambda b, h: (b, h)),
                  pl.BlockSpec((L, HEAD_DIM), lambda b, h: (b, h)),
                  pl.BlockSpec((L, HEAD_DIM), lambda b, h: (b, COL_AZ + h))],
        out_specs=pl.BlockSpec((L, HEAD_DIM), lambda b, h: (b, h)),
        out_shape=jax.ShapeDtypeStruct((B * L, D_MLA), BF16),
        compiler_params=_cparams("parallel", "parallel"), name="attn_prompt")(q, k, v, proj)


DECODE_SLOTS = 3


def _decode_kernel(pt_ref, qs_ref, cn_ref, kn_ref, ckv_hbm, krt_hbm, o_ref, ckv_buf, krt_buf, sem,
                   *, layer, n_pages, page, n_seq):
    i = pl.program_id(0)

    def page_copies(seq, slot):
        out = []
        for p in range(n_pages):
            pg = pt_ref[seq, p]
            rows = pl.ds(p * page, page)
            out.append(pltpu.make_async_copy(ckv_hbm.at[layer, pg], ckv_buf.at[slot, rows, :], sem.at[0, slot]))
            out.append(pltpu.make_async_copy(krt_hbm.at[layer, pg], krt_buf.at[slot, :, rows], sem.at[1, slot]))
        return out

    def start_fetch(seq, slot):
        for cp in page_copies(seq, slot):
            cp.start()

    def wait_fetch(seq, slot):
        for cp in page_copies(seq, slot):
            cp.wait()

    ahead = DECODE_SLOTS - 1

    @pl.when(i == 0)
    def _():
        for j in range(min(ahead, n_seq)):
            start_fetch(j, j)

    @pl.when(i + ahead < n_seq)
    def _():
        start_fetch(i + ahead, lax.rem(i + ahead, DECODE_SLOTS))

    slot = lax.rem(i, DECODE_SLOTS)
    wait_fetch(i, slot)

    q = qs_ref[0]
    q16 = jnp.concatenate([q, jnp.zeros_like(q)], axis=0).astype(BF16)
    ql = q16[:, 0:KV_RANK]
    qr = q16[:, KV_RANK:KV_RANK + D_ROPE]
    kc = ckv_buf[slot].astype(BF16)
    sc = _dot_nt(ql, kc) + _dot(qr, krt_buf[slot].astype(BF16))
    cn = cn_ref[0].astype(BF16).astype(F32)
    kn = kn_ref[0].astype(BF16).astype(F32)
    s_new = (jnp.sum(ql.astype(F32) * cn, axis=1, keepdims=True)
             + jnp.sum(qr.astype(F32) * kn, axis=1, keepdims=True))
    m = jnp.maximum(jnp.max(sc, axis=1, keepdims=True), s_new)
    p = jnp.exp(sc - m)
    p_new = jnp.exp(s_new - m)
    l = jnp.sum(p, axis=1, keepdims=True) + p_new
    pb = p.astype(BF16)
    half = kc.shape[0] // 2
    acc = (_dot(pb[:, :half], kc[:half]) + _dot(pb[:, half:], kc[half:])
           + p_new.astype(BF16).astype(F32) * cn)
    o_ref[0] = (acc / l)[0:H_MLA, :]


def _decode(page_table, qs, ckv_new, kr_new, cache_ckv, cache_krope_t, layer):
    DB, n_pages = page_table.shape
    page = cache_ckv.shape[2]
    P = n_pages * page
    kern = functools.partial(_decode_kernel, layer=layer, n_pages=n_pages, page=page, n_seq=DB)
    grid_spec = pltpu.PrefetchScalarGridSpec(
        num_scalar_prefetch=1, grid=(DB,),
        in_specs=[pl.BlockSpec((1, H_MLA, KV_RANK + LANES), lambda b, pt: (b, 0, 0)),
                  pl.BlockSpec((1, 1, KV_RANK), lambda b, pt: (b, 0, 0)),
                  pl.BlockSpec((1, 1, D_ROPE), lambda b, pt: (b, 0, 0)),
                  pl.BlockSpec(memory_space=pl.ANY), pl.BlockSpec(memory_space=pl.ANY)],
        out_specs=pl.BlockSpec((1, H_MLA, KV_RANK), lambda b, pt: (b, 0, 0)),
        scratch_shapes=[pltpu.VMEM((DECODE_SLOTS, P, KV_RANK), F32), pltpu.VMEM((DECODE_SLOTS, D_ROPE, P), F32),
                        pltpu.SemaphoreType.DMA((2, DECODE_SLOTS))])
    return pl.pallas_call(
        kern, grid_spec=grid_spec, out_shape=jax.ShapeDtypeStruct((DB, H_MLA, KV_RANK), F32),
        compiler_params=_cparams("arbitrary"), name="mla_decode")(
            page_table, qs, ckv_new.reshape(DB, 1, KV_RANK), kr_new.reshape(DB, 1, D_ROPE), cache_ckv, cache_krope_t)


def _mla_out_kernel(ol_ref, w_ref, z_ref, o_ref):
    o = _dot(ol_ref[0].astype(BF16), w_ref[...])
    o_ref[...] = (o * _silu(z_ref[...])).astype(o_ref.dtype)


def _mla_out(o_lat_h, wuv, layer, proj):
    T = proj.shape[0]
    return pl.pallas_call(
        _mla_out_kernel, grid=(H_MLA,),
        in_specs=[pl.BlockSpec((1, T, KV_RANK), lambda h: (h, 0, 0)),
                  pl.BlockSpec((None, None, KV_RANK, HEAD_DIM), lambda h: (layer, h, 0, 0)),
                  pl.BlockSpec((T, HEAD_DIM), lambda h: (0, COL_AZ + h))],
        out_specs=pl.BlockSpec((T, HEAD_DIM), lambda h: (0, h)),
        out_shape=jax.ShapeDtypeStruct((T, D_MLA), BF16),
        compiler_params=_cparams("parallel"), name="mla_out")(o_lat_h, wuv, proj)


def _out_proj_kernel(or_ref, om_ref, oa_ref, x_ref, w_ref, g_ref, *out_refs, last):
    x = (x_ref[...] + _dot(or_ref[...], w_ref[0:D_RET, :]) + _dot(om_ref[...], w_ref[D_RET:D_RET + D_ML, :])
         + _dot(oa_ref[...], w_ref[D_RET + D_ML:, :]))
    hn = _rms(x, g_ref[...])
    if last:
        out_refs[0][...] = hn
    else:
        out_refs[0][...] = x
        out_refs[1][...] = hn.astype(out_refs[1].dtype)


def _out_proj(o_ret, o_ml, o_mla, x, w, layer, g, last):
    T, D = x.shape
    tm = min(T, 512)
    row = lambda n: pl.BlockSpec((tm, n), lambda i: (i, 0))
    if last:
        out_specs, out_shape = [row(D)], [jax.ShapeDtypeStruct((T, D), F32)]
    else:
        out_specs = [row(D), row(D)]
        out_shape = [jax.ShapeDtypeStruct((T, D), F32), jax.ShapeDtypeStruct((T, D), BF16)]
    return pl.pallas_call(
        functools.partial(_out_proj_kernel, last=last), grid=(T // tm,),
        in_specs=[row(D_RET), row(D_ML), row(D_MLA), row(D),
                  pl.BlockSpec((None,) + w.shape[1:], lambda i: (layer, 0, 0)),
                  pl.BlockSpec((1, D), lambda i: (0, 0))],
        out_specs=out_specs, out_shape=out_shape,
        compiler_params=_cparams("parallel"), name="out_proj")(o_ret, o_ml, o_mla, x, w, g.reshape(1, D))


def _rope_tables(pos):
    pos = pos.astype(F32)[:, None]

    def cs(half):
        inv = ROPE_BASE ** (-jnp.arange(half, dtype=F32) / half)
        ang = pos * inv[None, :]
        return jnp.cos(ang), jnp.sin(ang)

    c64, s64 = cs(HEAD_DIM // 2)
    tab128 = jnp.stack([jnp.concatenate([c64, c64], -1), jnp.concatenate([-s64, s64], -1)])
    c32, s32 = cs(D_ROPE // 2)
    z32 = jnp.zeros_like(c32)
    tab64 = jnp.stack([jnp.concatenate([c32, c32, z32, z32], -1),
                       jnp.concatenate([-s32, z32, z32, z32], -1),
                       jnp.concatenate([z32, s32, z32, z32], -1)])
    return tab128, tab64


def _layout_w_in(w_in):
    wt = jnp.swapaxes(w_in, 1, 2)
    o = np.cumsum((0,) + IN_WIDTHS)
    seg = lambda a, b: wt[:, o[a]:o[b], :]
    pad = jnp.zeros((wt.shape[0], N_IN_PAD - D_IN, wt.shape[2]), wt.dtype)
    return jnp.concatenate([seg(0, 9), seg(11, 12), seg(12, 13), seg(14, 15), seg(13, 14), seg(9, 11), pad],
                           axis=1).astype(BF16)


def kernel(x_prompt, x_sample, state_ret, state_mlstm_c, state_mlstm_n, state_mlstm_m, cache_ckv, cache_krope,
           page_table, g_norm, w_in, b_ig, b_fg, g_ret, g_ml, g_q, g_kv, w_uq, w_uk, w_uv, w_out, g_final):
    B, L, D = x_prompt.shape
    DB, Ls, _ = x_sample.shape
    assert Ls == 1, "the sample group is one new token per sequence"
    depth = w_in.shape[0]
    past_len = page_table.shape[1] * cache_ckv.shape[2]

    tab128_p, tab64_p = _rope_tables(jnp.arange(L))
    tab128_s, tab64_s = _rope_tables(past_len + jnp.arange(Ls))
    tab128_s = jnp.broadcast_to(tab128_s, (2, DB, HEAD_DIM))
    tab64_s = jnp.broadcast_to(tab64_s, (3, DB, LANES))
    w_in_b = _layout_w_in(w_in)
    w_out_b = w_out.astype(BF16)
    w_uq_t = jnp.transpose(w_uq, (0, 2, 3, 1))
    wqn = w_uq_t[:, :, :D_NOPE, :].reshape(depth, D_MLA, Q_RANK).astype(BF16)
    wqr = jnp.concatenate([w_uq_t[:, :, D_NOPE:, :], jnp.zeros((depth, H_MLA, LANES - D_ROPE, Q_RANK), w_uq.dtype)],
                          2).reshape(depth, D_MLA, Q_RANK).astype(BF16)
    wk = w_uk.reshape(depth, KV_RANK, D_MLA).astype(BF16)
    wv = w_uv.reshape(depth, KV_RANK, D_MLA).astype(BF16)
    wukt = jnp.transpose(w_uk, (0, 2, 3, 1)).astype(BF16)
    wuv_h = jnp.transpose(w_uv, (0, 2, 1, 3)).astype(BF16)
    zeros_gate = jnp.zeros((depth, LANES - D_ROPE - 2 * H_ML), F32)
    gate_bias = jnp.concatenate([jnp.zeros((depth, D_ROPE), F32), b_ig, b_fg, zeros_gate], -1)

    cache_krope_t = jnp.swapaxes(cache_krope, 2, 3)

    xp = x_prompt.reshape(B * L, D)
    xs = x_sample.reshape(DB, D)
    hp = _rmsnorm(xp, g_norm[0], BF16)
    hs = _rmsnorm(xs, g_norm[0], BF16)
    p_out = [[] for _ in range(6)]
    s_out = [[] for _ in range(4)]
    s_ret_all = s_c_all = None
    for l in range(depth):
        last = l == depth - 1
        g_next = g_final if last else g_norm[l + 1]
        gq, gkv = g_q[l].reshape(1, Q_RANK), g_kv[l].reshape(1, KV_RANK)
        gr, gm = g_ret[l].reshape(1, D_RET), g_ml[l].reshape(1, D_ML)
        gb = gate_bias[l].reshape(1, LANES)

        proj = tail = _in_proj(hp, w_in_b, l, N_IN_PAD)
        o_ret, st_ret = _ret_prompt(proj, tab128_p, gr, B, L)
        o_ml, st_c, st_n, st_m = _mlstm_prompt(proj, tail, gb, gm, B, L)
        ckvn, krope, q, k, v = _mla_prep(tail, tab64_p, gq, gkv, wqn, wqr, (wk, wv), l, sample=False)
        o_mla = _attn_prompt(q, k, v, tail, B, L)
        res = _out_proj(o_ret, o_ml, o_mla, xp, w_out_b, l, g_next, last)
        if last:
            y_prompt = res[0]
        else:
            xp, hp = res
        for j, t in enumerate((st_ret, st_c, st_n[:, :, 0, :], st_m[:, :, 0, 0],
                               ckvn.reshape(B, L, KV_RANK), krope.reshape(B, L, D_ROPE))):
            p_out[j].append(t)

        proj_s = tail_s = _in_proj(hs, w_in_b, l, N_IN_PAD)
        o_ret, s_ret_all = _ret_step(proj_s, tab128_s, gr, state_ret, s_ret_all, l)
        o_ml, s_c_all, st_n, st_m = _mlstm_step(proj_s, tail_s, gb, gm, state_mlstm_c, state_mlstm_n,
                                                state_mlstm_m, s_c_all, l)
        ckvn, krope, qs = _mla_prep(tail_s, tab64_s, gq, gkv, wqn, wqr, (wukt,), l, sample=True)
        o_lat = _decode(page_table, jnp.transpose(qs, (1, 0, 2)), ckvn, krope, cache_ckv, cache_krope_t, l)
        o_mla = _mla_out(jnp.transpose(o_lat, (1, 0, 2)), wuv_h, l, tail_s)
        res = _out_proj(o_ret, o_ml, o_mla, xs, w_out_b, l, g_next, last)
        if last:
            y_sample = res[0]
        else:
            xs, hs = res
        for j, t in enumerate((st_n, st_m.reshape(DB, H_ML, HEAD_DIM)[:, :, 0],
                               ckvn.reshape(DB, Ls, KV_RANK), krope.reshape(DB, Ls, D_ROPE))):
            s_out[j].append(t)

    return (y_prompt.reshape(B, L, D), y_sample.reshape(DB, Ls, D),
            *[jnp.stack(t) for t in p_out], s_ret_all, s_c_all, *[jnp.stack(t) for t in s_out])
```

```python
import functools

import numpy as np
import jax
import jax.numpy as jnp
from jax import lax
from jax.experimental import pallas as pl
from jax.experimental.pallas import tpu as pltpu

F32 = jnp.float32
BF16 = jnp.bfloat16

HEAD_DIM = 128
H_RET = 4
H_ML = 4
H_MLA = 8
D_RET = H_RET * HEAD_DIM
D_ML = H_ML * HEAD_DIM
D_MLA = H_MLA * HEAD_DIM
Q_RANK = 512
KV_RANK = 256
D_NOPE = 128
D_ROPE = 64
ROPE_BASE = 10000.0
MIX_CHUNK = 128
NORM_EPS = 1e-6
MLSTM_EPS = 1e-6
ATT_SCALE = (D_NOPE + D_ROPE) ** -0.5
QK_SCALE = HEAD_DIM ** -0.5
IN_WIDTHS = (D_RET, D_RET, D_RET, D_RET, D_ML, D_ML, D_ML, D_ML, D_ML, H_ML, H_ML,
             Q_RANK, KV_RANK, D_ROPE, D_MLA)
D_IN = sum(IN_WIDTHS)

LANES = 128
VMEM_LIMIT_BYTES = 56 * 1024 * 1024

COL_RET = 0
COL_ML = 16
COL_CQ = 36
COL_CKV = 40
COL_AZ = 42
COL_GATE = 50
GATE_I_LANE = D_ROPE
GATE_F_LANE = D_ROPE + H_ML
N_IN_PAD = 52 * LANES
RET_LOG_G = [float(np.log1p(-np.exp2(np.float32(-5.0 - h)))) for h in range(H_RET)]


def _cparams(*sem):
    return pltpu.CompilerParams(dimension_semantics=sem, vmem_limit_bytes=VMEM_LIMIT_BYTES)


def _dot(a, b):
    return jnp.dot(a, b, preferred_element_type=F32)


def _dot_nt(a, b):
    return lax.dot_general(a, b, (((1,), (1,)), ((), ())), preferred_element_type=F32)


def _silu(x):
    return x * jax.nn.sigmoid(x)


def _rms(x, g):
    return x * lax.rsqrt(jnp.mean(x * x, axis=-1, keepdims=True) + NORM_EPS) * g


def _head_norm(o, g):
    mu = jnp.mean(o, axis=-1, keepdims=True)
    d = o - mu
    var = jnp.mean(d * d, axis=-1, keepdims=True)
    return d * lax.rsqrt(var + NORM_EPS) * g


def _rope128(x, cos, sin_signed):
    return x * cos + pltpu.roll(x, HEAD_DIM // 2, axis=1) * sin_signed


def _rope128b(x, cos, sin_signed):
    return x * cos + pltpu.roll(x, HEAD_DIM // 2, axis=2) * sin_signed


def _rope64(x, tab):
    q = D_ROPE // 2
    return x * tab[0] + pltpu.roll(x, LANES - q, axis=1) * tab[1] + pltpu.roll(x, q, axis=1) * tab[2]


def _rmsnorm_kernel(x_ref, g_ref, o_ref):
    o_ref[...] = _rms(x_ref[...], g_ref[...]).astype(o_ref.dtype)


def _rmsnorm(x, g, dtype):
    T, D = x.shape
    tm = min(T, 512)
    return pl.pallas_call(
        _rmsnorm_kernel, grid=(T // tm,),
        in_specs=[pl.BlockSpec((tm, D), lambda i: (i, 0)), pl.BlockSpec((1, D), lambda i: (0, 0))],
        out_specs=pl.BlockSpec((tm, D), lambda i: (i, 0)),
        out_shape=jax.ShapeDtypeStruct((T, D), dtype),
        compiler_params=_cparams("parallel"), name="rmsnorm")(x, g.reshape(1, D))


def _matmul_nt_kernel(h_ref, wt_ref, o_ref):
    o_ref[...] = _dot_nt(h_ref[...], wt_ref[...])


def _in_proj(hn, wt, layer, N):
    T, D = hn.shape
    tm = min(T, 2048)
    tn = 4 * LANES
    return pl.pallas_call(
        _matmul_nt_kernel, grid=(T // tm, N // tn),
        in_specs=[pl.BlockSpec((tm, D), lambda i, j: (i, 0)),
                  pl.BlockSpec((None, tn, D), lambda i, j: (layer, j, 0))],
        out_specs=pl.BlockSpec((tm, tn), lambda i, j: (i, j)),
        out_shape=jax.ShapeDtypeStruct((T, N), F32),
        compiler_params=_cparams("parallel", "arbitrary"), name="in_proj")(hn, wt)


def _ret_prompt_kernel(q_ref, k_ref, v_ref, z_ref, cs_ref, g_ref, o_ref, s_ref, S_scr):
    ci = pl.program_id(0)
    B, c = q_ref.shape[0], q_ref.shape[1]

    @pl.when(ci == 0)
    def _():
        S_scr[...] = jnp.zeros_like(S_scr)

    cos = cs_ref[0]
    sin = cs_ref[1]
    row = lax.broadcasted_iota(jnp.int32, (c, c), 0)
    col = lax.broadcasted_iota(jnp.int32, (c, c), 1)
    rel = (row - col).astype(F32)
    idx = lax.broadcasted_iota(jnp.int32, (c, 1), 0).astype(F32)
    for h in range(H_RET):
        sl = slice(h * HEAD_DIM, (h + 1) * HEAD_DIM)
        lg = RET_LOG_G[h]
        decay = jnp.where(rel >= 0, jnp.exp(lg * jnp.maximum(rel, 0.0)), 0.0)
        q_dec = jnp.exp(lg * (idx + 1.0))
        k_dec = jnp.exp(lg * (c - 1.0 - idx))
        bmm = lambda e, x, y: jnp.einsum(e, x, y, preferred_element_type=F32)
        q = _rope128b(q_ref[:, :, sl], cos, sin)
        k = _rope128b(k_ref[:, :, sl], cos, sin) * QK_SCALE
        vb = v_ref[:, :, sl].astype(BF16)
        qb = q.astype(BF16)
        S = S_scr[:, h]
        sc = bmm('bik,bjk->bij', qb, k.astype(BF16)) * decay
        o = bmm('bij,bjv->biv', sc.astype(BF16), vb) + bmm('bik,bkv->biv', qb, S.astype(BF16)) * q_dec
        kdt = jnp.swapaxes(k * k_dec, 1, 2).astype(BF16)
        S_scr[:, h] = S * float(np.exp(lg * c)) + bmm('bkj,bjv->bkv', kdt, vb)
        y = _head_norm(o, g_ref[:, sl]) * _silu(z_ref[:, :, sl])
        o_ref[:, :, sl] = y.astype(o_ref.dtype)

    @pl.when(ci == pl.num_programs(0) - 1)
    def _():
        s_ref[...] = S_scr[...]


def _ret_prompt(proj, cs, g, B, L):
    c = MIX_CHUNK if L % MIX_CHUNK == 0 else L
    wb = D_RET // LANES
    col = lambda j: (lambda i: (0, i, COL_RET // wb + j))
    proj3 = proj.reshape(B, L, proj.shape[1])
    o, st = pl.pallas_call(
        _ret_prompt_kernel, grid=(L // c,),
        in_specs=[pl.BlockSpec((B, c, D_RET), col(0)), pl.BlockSpec((B, c, D_RET), col(1)),
                  pl.BlockSpec((B, c, D_RET), col(2)), pl.BlockSpec((B, c, D_RET), col(3)),
                  pl.BlockSpec((2, c, HEAD_DIM), lambda i: (0, i, 0)),
                  pl.BlockSpec((1, D_RET), lambda i: (0, 0))],
        out_specs=[pl.BlockSpec((B, c, D_RET), lambda i: (0, i, 0)),
                   pl.BlockSpec((B, H_RET, HEAD_DIM, HEAD_DIM), lambda i: (0, 0, 0, 0))],
        out_shape=[jax.ShapeDtypeStruct((B, L, D_RET), BF16),
                   jax.ShapeDtypeStruct((B, H_RET, HEAD_DIM, HEAD_DIM), F32)],
        scratch_shapes=[pltpu.VMEM((B, H_RET, HEAD_DIM, HEAD_DIM), F32)],
        compiler_params=_cparams("arbitrary"), name="ret_prompt")(proj3, proj3, proj3, proj3, cs, g)
    return o.reshape(B * L, D_RET), st


def _log_sigmoid(x):
    return -(jnp.maximum(-x, 0.0) + jnp.log1p(jnp.exp(-jnp.abs(x))))


def _mlstm_prompt_kernel(q_ref, k_ref, v_ref, og_ref, z_ref, gate_ref, bias_ref, g_ref,
                         o_ref, c_ref, n_ref, m_ref, C_scr, M_scr):
    ci = pl.program_id(0)
    B, c = q_ref.shape[0], q_ref.shape[1]

    @pl.when(ci == 0)
    def _():
        C_scr[...] = jnp.zeros_like(C_scr)
        M_scr[...] = jnp.zeros_like(M_scr)

    row = lax.broadcasted_iota(jnp.int32, (c, c), 0)
    col = lax.broadcasted_iota(jnp.int32, (c, c), 1)
    causal = col <= row
    eye = col == row
    ones = jnp.ones((B, c, HEAD_DIM), BF16)
    gates = gate_ref[...] + bias_ref[...]
    bmm = lambda e, x, y: jnp.einsum(e, x, y, preferred_element_type=F32)
    for h in range(H_ML):
        sl = slice(h * HEAD_DIM, (h + 1) * HEAD_DIM)
        q = q_ref[:, :, sl]
        k = k_ref[:, :, sl] * QK_SCALE
        vb1 = jnp.concatenate([v_ref[:, :, sl].astype(BF16), ones], axis=2)
        i_col = gates[:, :, GATE_I_LANE + h:GATE_I_LANE + h + 1]
        lf_col = _log_sigmoid(gates[:, :, GATE_F_LANE + h:GATE_F_LANE + h + 1])
        lf_row = jnp.sum(jnp.where(eye, lf_col, 0.0), axis=1, keepdims=True)
        i_row = jnp.sum(jnp.where(eye, i_col, 0.0), axis=1, keepdims=True)
        b_col = jnp.sum(jnp.where(causal, lf_row, 0.0), axis=2, keepdims=True)
        b_row = jnp.sum(jnp.where(row <= col, lf_col, 0.0), axis=1, keepdims=True)
        m_prev = M_scr[:, h][:, :, 0:1]
        log_d = jnp.where(causal, b_col - b_row + i_row, -jnp.inf)
        inter = b_col + m_prev
        m_t = jnp.maximum(inter, jnp.max(log_d, axis=2, keepdims=True))
        w_intra = jnp.exp(log_d - m_t)
        w_inter = jnp.exp(inter - m_t)
        qb = q.astype(BF16)
        C1 = C_scr[:, h]
        s = bmm('bik,bjk->bij', qb, k.astype(BF16)) * w_intra
        a = bmm('bij,bjv->biv', s.astype(BF16), vb1) + w_inter * bmm('bik,bkv->biv', qb, C1.astype(BF16))
        hb = a[:, :, :HEAD_DIM] / (jnp.maximum(jnp.abs(a[:, :, HEAD_DIM:]), jnp.exp(-m_t)) + MLSTM_EPS)
        m_new = m_t[:, c - 1:c, :]
        b_last = b_col[:, c - 1:c, :]
        w_k = jnp.exp(b_last - b_col + i_col - m_new)
        dec = jnp.exp(b_last + m_prev - m_new)
        kwt = jnp.swapaxes(k * w_k, 1, 2).astype(BF16)
        C_scr[:, h] = dec * C1 + bmm('bkj,bjv->bkv', kwt, vb1)
        M_scr[:, h] = jnp.broadcast_to(m_new, (B, 1, HEAD_DIM))
        y = _head_norm(hb * jax.nn.sigmoid(og_ref[:, :, sl]), g_ref[:, sl]) * _silu(z_ref[:, :, sl])
        o_ref[:, :, sl] = y.astype(o_ref.dtype)

    @pl.when(ci == pl.num_programs(0) - 1)
    def _():
        m_ref[...] = M_scr[...]
        for b in range(B):
            for h in range(H_ML):
                C1 = C_scr[b, h]
                c_ref[b, h] = C1[:, :HEAD_DIM]
                n_ref[b, h] = C1[:, HEAD_DIM:].T[0:1, :]


def _mlstm_prompt(proj, proj_tail, gate_bias, g, B, L):
    c = MIX_CHUNK if L % MIX_CHUNK == 0 else L
    wb = D_ML // LANES
    col = lambda j: (lambda i: (0, i, COL_ML // wb + j))
    proj3 = proj.reshape(B, L, proj.shape[1])
    tail3 = proj_tail.reshape(B, L, proj_tail.shape[1])
    vec = jax.ShapeDtypeStruct((B, H_ML, 1, HEAD_DIM), F32)
    vec_spec = pl.BlockSpec((B, H_ML, 1, HEAD_DIM), lambda i: (0, 0, 0, 0))
    o, st_c, st_n, st_m = pl.pallas_call(
        _mlstm_prompt_kernel, grid=(L // c,),
        in_specs=[pl.BlockSpec((B, c, D_ML), col(0)), pl.BlockSpec((B, c, D_ML), col(1)),
                  pl.BlockSpec((B, c, D_ML), col(2)), pl.BlockSpec((B, c, D_ML), col(3)),
                  pl.BlockSpec((B, c, D_ML), col(4)),
                  pl.BlockSpec((B, c, LANES), lambda i: (0, i, COL_GATE)),
                  pl.BlockSpec((1, LANES), lambda i: (0, 0)),
                  pl.BlockSpec((1, D_ML), lambda i: (0, 0))],
        out_specs=[pl.BlockSpec((B, c, D_ML), lambda i: (0, i, 0)),
                   pl.BlockSpec((B, H_ML, HEAD_DIM, HEAD_DIM), lambda i: (0, 0, 0, 0)),
                   vec_spec, vec_spec],
        out_shape=[jax.ShapeDtypeStruct((B, L, D_ML), BF16),
                   jax.ShapeDtypeStruct((B, H_ML, HEAD_DIM, HEAD_DIM), F32), vec, vec],
        scratch_shapes=[pltpu.VMEM((B, H_ML, HEAD_DIM, 2 * HEAD_DIM), F32),
                        pltpu.VMEM((B, H_ML, 1, HEAD_DIM), F32)],
        compiler_params=_cparams("arbitrary"),
        name="mlstm_prompt")(proj3, proj3, proj3, proj3, proj3, tail3, gate_bias, g)
    return o.reshape(B * L, D_ML), st_c, st_n, st_m


STEP_TOKENS = 16


def _outer_t(kb, vm):
    return lax.dot_general(kb, vm, (((0,), (0,)), ((), ())), preferred_element_type=F32)


def _ret_step_kernel(q_ref, k_ref, v_ref, z_ref, cs_ref, g_ref, s_ref, *rest):
    o_ref, so_ref = rest[-2:]
    tb = q_ref.shape[0]
    cos = cs_ref[0]
    sin = cs_ref[1]
    tok = lax.broadcasted_iota(jnp.int32, (tb, HEAD_DIM), 0)
    for h in range(H_RET):
        sl = slice(h * HEAD_DIM, (h + 1) * HEAD_DIM)
        gamma = float(np.exp(RET_LOG_G[h]))
        q = _rope128(q_ref[:, sl], cos, sin)
        k = _rope128(k_ref[:, sl], cos, sin) * QK_SCALE
        v = v_ref[:, sl]
        qb = q.astype(BF16)
        kb = k.astype(BF16)
        ro = jnp.zeros((tb, HEAD_DIM), F32)
        for t in range(tb):
            S = s_ref[0, t, h]
            ro = jnp.where(tok == t, _dot(qb, S.astype(BF16)), ro)
            vm = jnp.where(tok == t, v, 0.0).astype(BF16)
            so_ref[0, t, h] = gamma * S + _outer_t(kb, vm)
        o = jnp.sum(q * k, axis=1, keepdims=True) * v + gamma * ro
        y = _head_norm(o, g_ref[:, sl]) * _silu(z_ref[:, sl])
        o_ref[:, sl] = y.astype(o_ref.dtype)


def _stacked_state_specs(prev, depth, T, tb, H, layer):
    shape = jax.ShapeDtypeStruct((depth, T, H, HEAD_DIM, HEAD_DIM), F32)
    spec = pl.BlockSpec((1, tb, H, HEAD_DIM, HEAD_DIM), lambda i: (layer, i, 0, 0, 0))
    if prev is None:
        prev = jnp.zeros(shape.shape, shape.dtype)
    return shape, spec, [pl.BlockSpec(memory_space=pl.ANY)], [prev]


def _ret_step(proj, cs, g, state, prev, layer):
    T = proj.shape[0]
    depth = state.shape[0]
    tb = min(T, STEP_TOKENS)
    wb = D_RET // LANES
    col = lambda j: (lambda i: (i, COL_RET // wb + j))
    st_shape, st_spec, extra_specs, extra_args = _stacked_state_specs(prev, depth, T, tb, H_RET, layer)
    n_in = 7
    return pl.pallas_call(
        _ret_step_kernel, grid=(T // tb,),
        in_specs=[pl.BlockSpec((tb, D_RET), col(0)), pl.BlockSpec((tb, D_RET), col(1)),
                  pl.BlockSpec((tb, D_RET), col(2)), pl.BlockSpec((tb, D_RET), col(3)),
                  pl.BlockSpec((2, tb, HEAD_DIM), lambda i: (0, i, 0)),
                  pl.BlockSpec((1, D_RET), lambda i: (0, 0)),
                  pl.BlockSpec((1, tb, H_RET, HEAD_DIM, HEAD_DIM), lambda i: (layer, i, 0, 0, 0))] + extra_specs,
        out_specs=[pl.BlockSpec((tb, D_RET), lambda i: (i, 0)), st_spec],
        out_shape=[jax.ShapeDtypeStruct((T, D_RET), BF16), st_shape],
        input_output_aliases={n_in: 1},
        compiler_params=_cparams("parallel"), name="ret_step")(proj, proj, proj, proj, cs, g, state, *extra_args)


def _mlstm_step_kernel(q_ref, k_ref, v_ref, og_ref, z_ref, gate_ref, bias_ref, g_ref, c_ref, n_ref, m_ref,
                       *rest):
    o_ref, co_ref, no_ref, mo_ref = rest[-4:]
    tb = q_ref.shape[0]
    gates = gate_ref[...] + bias_ref[...]
    m_all = m_ref[0]
    tok = lax.broadcasted_iota(jnp.int32, (tb, HEAD_DIM), 0)
    for h in range(H_ML):
        sl = slice(h * HEAD_DIM, (h + 1) * HEAD_DIM)
        q = q_ref[:, sl]
        k = k_ref[:, sl] * QK_SCALE
        v = v_ref[:, sl]
        ib = gates[:, GATE_I_LANE + h:GATE_I_LANE + h + 1]
        lf = _log_sigmoid(gates[:, GATE_F_LANE + h:GATE_F_LANE + h + 1])
        inter = lf + m_all[:, h:h + 1]
        m_t = jnp.maximum(inter, ib)
        w_intra = jnp.exp(ib - m_t)
        w_inter = jnp.exp(inter - m_t)
        nv = n_ref[0, :, h, :]
        qb = q.astype(BF16)
        kb = k.astype(BF16)
        vw = v * w_intra
        ro = jnp.zeros((tb, HEAD_DIM), F32)
        for t in range(tb):
            Cm = c_ref[0, t, h]
            ro = jnp.where(tok == t, _dot(qb, Cm.astype(BF16)), ro)
            vm = jnp.where(tok == t, vw, 0.0).astype(BF16)
            co_ref[0, t, h] = w_inter[t:t + 1, :] * Cm + _outer_t(kb, vm)
        s = jnp.sum(q * k, axis=1, keepdims=True) * w_intra
        num = s * v + w_inter * ro
        qn = s + w_inter * jnp.sum(q * nv, axis=1, keepdims=True)
        hb = num / (jnp.maximum(jnp.abs(qn), jnp.exp(-m_t)) + MLSTM_EPS)
        no_ref[:, h, :] = w_inter * nv + w_intra * k
        mo_ref[:, sl] = jnp.broadcast_to(m_t, (tb, HEAD_DIM))
        y = _head_norm(hb * jax.nn.sigmoid(og_ref[:, sl]), g_ref[:, sl]) * _silu(z_ref[:, sl])
        o_ref[:, sl] = y.astype(o_ref.dtype)


def _mlstm_step(proj, proj_tail, gate_bias, g, c_state, n_state, m_state, prev, layer):
    T = proj.shape[0]
    depth = c_state.shape[0]
    tb = min(T, STEP_TOKENS)
    wb = D_ML // LANES
    col = lambda j: (lambda i: (i, COL_ML // wb + j))
    st_shape, st_spec, extra_specs, extra_args = _stacked_state_specs(prev, depth, T, tb, H_ML, layer)
    n_in = 11
    return pl.pallas_call(
        _mlstm_step_kernel, grid=(T // tb,),
        in_specs=[pl.BlockSpec((tb, D_ML), col(0)), pl.BlockSpec((tb, D_ML), col(1)),
                  pl.BlockSpec((tb, D_ML), col(2)), pl.BlockSpec((tb, D_ML), col(3)),
                  pl.BlockSpec((tb, D_ML), col(4)),
                  pl.BlockSpec((tb, LANES), lambda i: (i, COL_GATE)),
                  pl.BlockSpec((1, LANES), lambda i: (0, 0)),
                  pl.BlockSpec((1, D_ML), lambda i: (0, 0)),
                  pl.BlockSpec((1, tb, H_ML, HEAD_DIM, HEAD_DIM), lambda i: (layer, i, 0, 0, 0)),
                  pl.BlockSpec((1, tb, H_ML, HEAD_DIM), lambda i: (layer, i, 0, 0)),
                  pl.BlockSpec((1, tb, H_ML), lambda i: (layer, i, 0))] + extra_specs,
        out_specs=[pl.BlockSpec((tb, D_ML), lambda i: (i, 0)), st_spec,
                   pl.BlockSpec((tb, H_ML, HEAD_DIM), lambda i: (i, 0, 0)),
                   pl.BlockSpec((tb, D_ML), lambda i: (i, 0))],
        out_shape=[jax.ShapeDtypeStruct((T, D_ML), BF16), st_shape,
                   jax.ShapeDtypeStruct((T, H_ML, HEAD_DIM), F32),
                   jax.ShapeDtypeStruct((T, D_ML), F32)],
        input_output_aliases={n_in: 1},
        compiler_params=_cparams("parallel"),
        name="mlstm_step")(proj, proj, proj, proj, proj, proj_tail, gate_bias, g, c_state, n_state, m_state,
                           *extra_args)


def _mla_common(cq_ref, ckv_ref, gate_ref, tab_ref, gq_ref, gkv_ref, wqn_ref, wqr_ref, ckvn_ref, kr_ref):
    tab = tab_ref[...]
    cqn = _rms(cq_ref[...], gq_ref[...]).astype(BF16)
    ckvn = _rms(ckv_ref[...], gkv_ref[...])
    ckvn_ref[...] = ckvn
    k_rope = _rope64(gate_ref[...], tab)
    kr_ref[...] = k_rope[:, :D_ROPE]
    q_nope = _dot_nt(cqn, wqn_ref[...])
    q_rope = _dot_nt(cqn, wqr_ref[...])
    return tab, ckvn, k_rope, q_nope, q_rope


def _mla_prep_prompt_kernel(cq_ref, ckv_ref, gate_ref, tab_ref, gq_ref, gkv_ref, wqn_ref, wqr_ref, wk_ref, wv_ref,
                            ckvn_ref, kr_ref, q_ref, k_ref, v_ref):
    tab, ckvn, k_rope, q_nope, q_rope = _mla_common(cq_ref, ckv_ref, gate_ref, tab_ref, gq_ref, gkv_ref,
                                                    wqn_ref, wqr_ref, ckvn_ref, kr_ref)
    cb = ckvn.astype(BF16)
    k_nope = _dot(cb, wk_ref[...])
    v_ref[...] = _dot(cb, wv_ref[...]).astype(v_ref.dtype)
    krb = k_rope.astype(k_ref.dtype)
    for h in range(H_MLA):
        sl = slice(h * HEAD_DIM, (h + 1) * HEAD_DIM)
        lo = slice(2 * h * LANES, (2 * h + 1) * LANES)
        hi = slice((2 * h + 1) * LANES, (2 * h + 2) * LANES)
        q_ref[:, lo] = (q_nope[:, sl] * ATT_SCALE).astype(q_ref.dtype)
        q_ref[:, hi] = (_rope64(q_rope[:, sl], tab) * ATT_SCALE).astype(q_ref.dtype)
        k_ref[:, lo] = k_nope[:, sl].astype(k_ref.dtype)
        k_ref[:, hi] = krb


def _mla_prep_sample_kernel(cq_ref, ckv_ref, gate_ref, tab_ref, gq_ref, gkv_ref, wqn_ref, wqr_ref, wukt_ref,
                            ckvn_ref, kr_ref, qs_ref):
    tab, ckvn, k_rope, q_nope, q_rope = _mla_common(cq_ref, ckv_ref, gate_ref, tab_ref, gq_ref, gkv_ref,
                                                    wqn_ref, wqr_ref, ckvn_ref, kr_ref)
    for h in range(H_MLA):
        sl = slice(h * HEAD_DIM, (h + 1) * HEAD_DIM)
        q_lat = _dot(q_nope[:, sl].astype(BF16), wukt_ref[h])
        qs_ref[h, :, 0:KV_RANK] = q_lat * ATT_SCALE
        qs_ref[h, :, KV_RANK:KV_RANK + LANES] = _rope64(q_rope[:, sl], tab) * ATT_SCALE


def _mla_prep(proj, tab, gq, gkv, wqn, wqr, extra_w, layer, sample):
    T = proj.shape[0]
    tm = min(T, 512)
    tab_blocks = tab.shape[1] // tm
    full = lambda a: pl.BlockSpec(a.shape, lambda i: (0,) * a.ndim)
    of_layer = lambda a: pl.BlockSpec((None,) + a.shape[1:], lambda i: (layer,) + (0,) * (a.ndim - 1))
    in_specs = [pl.BlockSpec((tm, Q_RANK), lambda i: (i, COL_CQ * LANES // Q_RANK)),
                pl.BlockSpec((tm, KV_RANK), lambda i: (i, COL_CKV * LANES // KV_RANK)),
                pl.BlockSpec((tm, LANES), lambda i: (i, COL_GATE)),
                pl.BlockSpec((3, tm, LANES), lambda i: (0, i % tab_blocks, 0)),
                full(gq), full(gkv), of_layer(wqn), of_layer(wqr)] + [of_layer(w) for w in extra_w]
    out_specs = [pl.BlockSpec((tm, KV_RANK), lambda i: (i, 0)), pl.BlockSpec((tm, D_ROPE), lambda i: (i, 0))]
    out_shape = [jax.ShapeDtypeStruct((T, KV_RANK), F32), jax.ShapeDtypeStruct((T, D_ROPE), F32)]
    if sample:
        kern = _mla_prep_sample_kernel
        out_specs.append(pl.BlockSpec((H_MLA, tm, KV_RANK + LANES), lambda i: (0, i, 0)))
        out_shape.append(jax.ShapeDtypeStruct((H_MLA, T, KV_RANK + LANES), F32))
    else:
        kern = _mla_prep_prompt_kernel
        out_specs += [pl.BlockSpec((tm, 2 * D_MLA), lambda i: (i, 0)), pl.BlockSpec((tm, 2 * D_MLA), lambda i: (i, 0)),
                      pl.BlockSpec((tm, D_MLA), lambda i: (i, 0))]
        out_shape += [jax.ShapeDtypeStruct((T, 2 * D_MLA), BF16), jax.ShapeDtypeStruct((T, 2 * D_MLA), BF16),
                      jax.ShapeDtypeStruct((T, D_MLA), BF16)]
    return pl.pallas_call(
        kern, grid=(T // tm,), in_specs=in_specs, out_specs=out_specs, out_shape=out_shape,
        compiler_params=_cparams("parallel"),
        name="mla_prep_sample" if sample else "mla_prep_prompt")(proj, proj, proj, tab, gq, gkv, wqn, wqr, *extra_w)


def _attn_prompt_kernel(q_ref, k_ref, v_ref, z_ref, o_ref, *, tq):
    L = q_ref.shape[0]
    n = L // tq
    row = lax.broadcasted_iota(jnp.int32, (tq, tq), 0)
    col = lax.broadcasted_iota(jnp.int32, (tq, tq), 1)

    def scores(i):
        cur = slice(i * tq, (i + 1) * tq)
        q = q_ref[cur, :]
        sd = _dot_nt(q, k_ref[cur, :])
        so = _dot_nt(q, k_ref[0:i * tq, :]) if i > 0 else None
        return sd, so

    def values(i, pd, po, l):
        cur = slice(i * tq, (i + 1) * tq)
        acc = _dot(pd, v_ref[cur, :])
        if i > 0:
            acc = acc + _dot(po, v_ref[0:i * tq, :])
        o_ref[cur, :] = (acc / l * _silu(z_ref[cur, :])).astype(o_ref.dtype)

    nxt = scores(0)
    prev = None
    for i in range(n):
        sd, so = nxt
        if i + 1 < n:
            nxt = scores(i + 1)
        if prev is not None:
            values(i - 1, *prev)
        sd = jnp.where(col <= row, sd, -jnp.inf)
        m = jnp.max(sd, axis=1, keepdims=True)
        if i > 0:
            m = jnp.maximum(m, jnp.max(so, axis=1, keepdims=True))
        pd = jnp.exp(sd - m)
        l = jnp.sum(pd, axis=1, keepdims=True)
        po = None
        if i > 0:
            po = jnp.exp(so - m)
            l = l + jnp.sum(po, axis=1, keepdims=True)
            po = po.astype(BF16)
        prev = (pd.astype(BF16), po, l)
    values(n - 1, *prev)


def _attn_prompt(q, k, v, proj, B, L):
    tq = min(L, 256)
    return pl.pallas_call(
        functools.partial(_attn_prompt_kernel, tq=tq), grid=(B, H_MLA),
        in_specs=[pl.BlockSpec((L, 2 * LANES), lambda b, h: (b, h)),
                  pl.BlockSpec((L, 2 * LANES), lambda b, h: (b, h)),
                  pl.BlockSpec((L, HEAD_DIM), lambda b, h: (b, h)),
                  pl.BlockSpec((L, HEAD_DIM), lambda b, h: (b, COL_AZ + h))],
        out_specs=pl.BlockSpec((L, HEAD_DIM), lambda b, h: (b, h)),
        out_shape=jax.ShapeDtypeStruct((B * L, D_MLA), BF16),
        compiler_params=_cparams("parallel", "parallel"), name="attn_prompt")(q, k, v, proj)


DECODE_SLOTS = 3


def _decode_kernel(pt_ref, qs_ref, cn_ref, kn_ref, ckv_hbm, krt_hbm, o_ref, ckv_buf, krt_buf, sem,
                   *, layer, n_pages, page, n_seq):
    i = pl.program_id(0)

    def page_copies(seq, slot):
        out = []
        for p in range(n_pages):
            pg = pt_ref[seq, p]
            rows = pl.ds(p * page, page)
            out.append(pltpu.make_async_copy(ckv_hbm.at[layer, pg], ckv_buf.at[slot, rows, :], sem.at[0, slot]))
            out.append(pltpu.make_async_copy(krt_hbm.at[layer, pg], krt_buf.at[slot, :, rows], sem.at[1, slot]))
        return out

    def start_fetch(seq, slot):
        for cp in page_copies(seq, slot):
            cp.start()

    def wait_fetch(seq, slot):
        for cp in page_copies(seq, slot):
            cp.wait()

    ahead = DECODE_SLOTS - 1

    @pl.when(i == 0)
    def _():
        for j in range(min(ahead, n_seq)):
            start_fetch(j, j)

    @pl.when(i + ahead < n_seq)
    def _():
        start_fetch(i + ahead, lax.rem(i + ahead, DECODE_SLOTS))

    slot = lax.rem(i, DECODE_SLOTS)
    wait_fetch(i, slot)

    q = qs_ref[0]
    q16 = jnp.concatenate([q, jnp.zeros_like(q)], axis=0).astype(BF16)
    ql = q16[:, 0:KV_RANK]
    qr = q16[:, KV_RANK:KV_RANK + D_ROPE]
    kc = ckv_buf[slot].astype(BF16)
    sc = _dot_nt(ql, kc) + _dot(qr, krt_buf[slot].astype(BF16))
    cn = cn_ref[0].astype(BF16).astype(F32)
    kn = kn_ref[0].astype(BF16).astype(F32)
    s_new = (jnp.sum(ql.astype(F32) * cn, axis=1, keepdims=True)
             + jnp.sum(qr.astype(F32) * kn, axis=1, keepdims=True))
    m = jnp.maximum(jnp.max(sc, axis=1, keepdims=True), s_new)
    p = jnp.exp(sc - m)
    p_new = jnp.exp(s_new - m)
    l = jnp.sum(p, axis=1, keepdims=True) + p_new
    pb = p.astype(BF16)
    half = kc.shape[0] // 2
    acc = (_dot(pb[:, :half], kc[:half]) + _dot(pb[:, half:], kc[half:])
           + p_new.astype(BF16).astype(F32) * cn)
    o_ref[0] = (acc / l)[0:H_MLA, :]


def _decode(page_table, qs, ckv_new, kr_new, cache_ckv, cache_krope_t, layer):
    DB, n_pages = page_table.shape
    page = cache_ckv.shape[2]
    P = n_pages * page
    kern = functools.partial(_decode_kernel, layer=layer, n_pages=n_pages, page=page, n_seq=DB)
    grid_spec = pltpu.PrefetchScalarGridSpec(
        num_scalar_prefetch=1, grid=(DB,),
        in_specs=[pl.BlockSpec((1, H_MLA, KV_RANK + LANES), lambda b, pt: (b, 0, 0)),
                  pl.BlockSpec((1, 1, KV_RANK), lambda b, pt: (b, 0, 0)),
                  pl.BlockSpec((1, 1, D_ROPE), lambda b, pt: (b, 0, 0)),
                  pl.BlockSpec(memory_space=pl.ANY), pl.BlockSpec(memory_space=pl.ANY)],
        out_specs=pl.BlockSpec((1, H_MLA, KV_RANK), lambda b, pt: (b, 0, 0)),
        scratch_shapes=[pltpu.VMEM((DECODE_SLOTS, P, KV_RANK), F32), pltpu.VMEM((DECODE_SLOTS, D_ROPE, P), F32),
                        pltpu.SemaphoreType.DMA((2, DECODE_SLOTS))])
    return pl.pallas_call(
        kern, grid_spec=grid_spec, out_shape=jax.ShapeDtypeStruct((DB, H_MLA, KV_RANK), F32),
        compiler_params=_cparams("arbitrary"), name="mla_decode")(
            page_table, qs, ckv_new.reshape(DB, 1, KV_RANK), kr_new.reshape(DB, 1, D_ROPE), cache_ckv, cache_krope_t)


def _mla_out_kernel(ol_ref, w_ref, z_ref, o_ref):
    o = _dot(ol_ref[0].astype(BF16), w_ref[...])
    o_ref[...] = (o * _silu(z_ref[...])).astype(o_ref.dtype)


def _mla_out(o_lat_h, wuv, layer, proj):
    T = proj.shape[0]
    return pl.pallas_call(
        _mla_out_kernel, grid=(H_MLA,),
        in_specs=[pl.BlockSpec((1, T, KV_RANK), lambda h: (h, 0, 0)),
                  pl.BlockSpec((None, None, KV_RANK, HEAD_DIM), lambda h: (layer, h, 0, 0)),
                  pl.BlockSpec((T, HEAD_DIM), lambda h: (0, COL_AZ + h))],
        out_specs=pl.BlockSpec((T, HEAD_DIM), lambda h: (0, h)),
        out_shape=jax.ShapeDtypeStruct((T, D_MLA), BF16),
        compiler_params=_cparams("parallel"), name="mla_out")(o_lat_h, wuv, proj)


def _out_proj_kernel(or_ref, om_ref, oa_ref, x_ref, w_ref, g_ref, *out_refs, last):
    x = (x_ref[...] + _dot(or_ref[...], w_ref[0:D_RET, :]) + _dot(om_ref[...], w_ref[D_RET:D_RET + D_ML, :])
         + _dot(oa_ref[...], w_ref[D_RET + D_ML:, :]))
    hn = _rms(x, g_ref[...])
    if last:
        out_refs[0][...] = hn
    else:
        out_refs[0][...] = x
        out_refs[1][...] = hn.astype(out_refs[1].dtype)


def _out_proj(o_ret, o_ml, o_mla, x, w, layer, g, last):
    T, D = x.shape
    tm = min(T, 512)
    row = lambda n: pl.BlockSpec((tm, n), lambda i: (i, 0))
    if last:
        out_specs, out_shape = [row(D)], [jax.ShapeDtypeStruct((T, D), F32)]
    else:
        out_specs = [row(D), row(D)]
        out_shape = [jax.ShapeDtypeStruct((T, D), F32), jax.ShapeDtypeStruct((T, D), BF16)]
    return pl.pallas_call(
        functools.partial(_out_proj_kernel, last=last), grid=(T // tm,),
        in_specs=[row(D_RET), row(D_ML), row(D_MLA), row(D),
                  pl.BlockSpec((None,) + w.shape[1:], lambda i: (layer, 0, 0)),
                  pl.BlockSpec((1, D), lambda i: (0, 0))],
        out_specs=out_specs, out_shape=out_shape,
        compiler_params=_cparams("parallel"), name="out_proj")(o_ret, o_ml, o_mla, x, w, g.reshape(1, D))


def _rope_tables(pos):
    pos = pos.astype(F32)[:, None]

    def cs(half):
        inv = ROPE_BASE ** (-jnp.arange(half, dtype=F32) / half)
        ang = pos * inv[None, :]
        return jnp.cos(ang), jnp.sin(ang)

    c64, s64 = cs(HEAD_DIM // 2)
    tab128 = jnp.stack([jnp.concatenate([c64, c64], -1), jnp.concatenate([-s64, s64], -1)])
    c32, s32 = cs(D_ROPE // 2)
    z32 = jnp.zeros_like(c32)
    tab64 = jnp.stack([jnp.concatenate([c32, c32, z32, z32], -1),
                       jnp.concatenate([-s32, z32, z32, z32], -1),
                       jnp.concatenate([z32, s32, z32, z32], -1)])
    return tab128, tab64


RELAYOUT_ROWS = 4 * LANES


def _w_in_block_plan():
    o = np.cumsum((0,) + IN_WIDTHS)
    order = [(0, 9), (11, 12), (12, 13), (14, 15), (13, 14), (9, 11)]
    plan = [[] for _ in range(N_IN_PAD // RELAYOUT_ROWS)]
    dst = 0
    for a, b in order:
        src, rows = int(o[a]), int(o[b] - o[a])
        while rows:
            blk, off = divmod(dst, RELAYOUT_ROWS)
            n = min(rows, RELAYOUT_ROWS - off)
            plan[blk].append((src, off, n))
            src, dst, rows = src + n, dst + n, rows - n
    return plan


def _relayout_kernel(wt_hbm, o_ref, buf, sem, *, plan):
    l = pl.program_id(0)
    j = pl.program_id(1)
    nb = len(plan)

    def copies(layer, blk, slot):
        return [pltpu.make_async_copy(wt_hbm.at[layer, pl.ds(src, n), :], buf.at[slot, pl.ds(off, n), :], sem.at[slot])
                for src, off, n in plan[blk]]

    g = l * nb + j
    slot = lax.rem(g, 2)

    @pl.when(g == 0)
    def _():
        for cp in copies(0, 0, 0):
            cp.start()

    for jj in range(nb):
        @pl.when(j == jj)
        def _(jj=jj):
            if jj + 1 < nb:
                for cp in copies(l, jj + 1, 1 - slot):
                    cp.start()
            else:
                @pl.when(l + 1 < pl.num_programs(0))
                def _():
                    for cp in copies(l + 1, 0, 1 - slot):
                        cp.start()
            for cp in copies(l, jj, slot):
                cp.wait()
            covered = sum(n for _, _, n in plan[jj])
            o_ref[0:covered, :] = buf[slot, 0:covered, :].astype(o_ref.dtype)
            if covered < RELAYOUT_ROWS:
                o_ref[covered:, :] = jnp.zeros((RELAYOUT_ROWS - covered, o_ref.shape[1]), o_ref.dtype)


def _layout_w_in(w_in):
    wt = jnp.swapaxes(w_in, 1, 2)
    depth, _, D = wt.shape
    plan = _w_in_block_plan()
    return pl.pallas_call(
        functools.partial(_relayout_kernel, plan=plan), grid=(depth, len(plan)),
        in_specs=[pl.BlockSpec(memory_space=pl.ANY)],
        out_specs=pl.BlockSpec((None, RELAYOUT_ROWS, D), lambda l, j: (l, j, 0)),
        out_shape=jax.ShapeDtypeStruct((depth, N_IN_PAD, D), BF16),
        scratch_shapes=[pltpu.VMEM((2, RELAYOUT_ROWS, D), F32), pltpu.SemaphoreType.DMA((2,))],
        compiler_params=_cparams("arbitrary", "arbitrary"), name="w_in_relayout")(wt)


def kernel(x_prompt, x_sample, state_ret, state_mlstm_c, state_mlstm_n, state_mlstm_m, cache_ckv, cache_krope,
           page_table, g_norm, w_in, b_ig, b_fg, g_ret, g_ml, g_q, g_kv, w_uq, w_uk, w_uv, w_out, g_final):
    B, L, D = x_prompt.shape
    DB, Ls, _ = x_sample.shape
    assert Ls == 1, "the sample group is one new token per sequence"
    depth = w_in.shape[0]
    past_len = page_table.shape[1] * cache_ckv.shape[2]

    tab128_p, tab64_p = _rope_tables(jnp.arange(L))
    tab128_s, tab64_s = _rope_tables(past_len + jnp.arange(Ls))
    tab128_s = jnp.broadcast_to(tab128_s, (2, DB, HEAD_DIM))
    tab64_s = jnp.broadcast_to(tab64_s, (3, DB, LANES))
    w_in_b = _layout_w_in(w_in)
    w_out_b = w_out.astype(BF16)
    w_uq_t = jnp.transpose(w_uq, (0, 2, 3, 1))
    wqn = w_uq_t[:, :, :D_NOPE, :].reshape(depth, D_MLA, Q_RANK).astype(BF16)
    wqr = jnp.concatenate([w_uq_t[:, :, D_NOPE:, :], jnp.zeros((depth, H_MLA, LANES - D_ROPE, Q_RANK), w_uq.dtype)],
                          2).reshape(depth, D_MLA, Q_RANK).astype(BF16)
    wk = w_uk.reshape(depth, KV_RANK, D_MLA).astype(BF16)
    wv = w_uv.reshape(depth, KV_RANK, D_MLA).astype(BF16)
    wukt = jnp.transpose(w_uk, (0, 2, 3, 1)).astype(BF16)
    wuv_h = jnp.transpose(w_uv, (0, 2, 1, 3)).astype(BF16)
    zeros_gate = jnp.zeros((depth, LANES - D_ROPE - 2 * H_ML), F32)
    gate_bias = jnp.concatenate([jnp.zeros((depth, D_ROPE), F32), b_ig, b_fg, zeros_gate], -1)

    cache_krope_t = jnp.swapaxes(cache_krope, 2, 3)

    xp = x_prompt.reshape(B * L, D)
    xs = x_sample.reshape(DB, D)
    hp = _rmsnorm(xp, g_norm[0], BF16)
    hs = _rmsnorm(xs, g_norm[0], BF16)
    p_out = [[] for _ in range(6)]
    s_out = [[] for _ in range(4)]
    s_ret_all = s_c_all = None
    for l in range(depth):
        last = l == depth - 1
        g_next = g_final if last else g_norm[l + 1]
        gq, gkv = g_q[l].reshape(1, Q_RANK), g_kv[l].reshape(1, KV_RANK)
        gr, gm = g_ret[l].reshape(1, D_RET), g_ml[l].reshape(1, D_ML)
        gb = gate_bias[l].reshape(1, LANES)

        proj = tail = _in_proj(hp, w_in_b, l, N_IN_PAD)
        o_ret, st_ret = _ret_prompt(proj, tab128_p, gr, B, L)
        o_ml, st_c, st_n, st_m = _mlstm_prompt(proj, tail, gb, gm, B, L)
        ckvn, krope, q, k, v = _mla_prep(tail, tab64_p, gq, gkv, wqn, wqr, (wk, wv), l, sample=False)
        o_mla = _attn_prompt(q, k, v, tail, B, L)
        res = _out_proj(o_ret, o_ml, o_mla, xp, w_out_b, l, g_next, last)
        if last:
            y_prompt = res[0]
        else:
            xp, hp = res
        for j, t in enumerate((st_ret, st_c, st_n[:, :, 0, :], st_m[:, :, 0, 0],
                               ckvn.reshape(B, L, KV_RANK), krope.reshape(B, L, D_ROPE))):
            p_out[j].append(t)

        proj_s = tail_s = _in_proj(hs, w_in_b, l, N_IN_PAD)
        o_ret, s_ret_all = _ret_step(proj_s, tab128_s, gr, state_ret, s_ret_all, l)
        o_ml, s_c_all, st_n, st_m = _mlstm_step(proj_s, tail_s, gb, gm, state_mlstm_c, state_mlstm_n,
                                                state_mlstm_m, s_c_all, l)
        ckvn, krope, qs = _mla_prep(tail_s, tab64_s, gq, gkv, wqn, wqr, (wukt,), l, sample=True)
        o_lat = _decode(page_table, jnp.transpose(qs, (1, 0, 2)), ckvn, krope, cache_ckv, cache_krope_t, l)
        o_mla = _mla_out(jnp.transpose(o_lat, (1, 0, 2)), wuv_h, l, tail_s)
        res = _out_proj(o_ret, o_ml, o_mla, xs, w_out_b, l, g_next, last)
        if last:
            y_sample = res[0]
        else:
            xs, hs = res
        for j, t in enumerate((st_n, st_m.reshape(DB, H_ML, HEAD_DIM)[:, :, 0],
                               ckvn.reshape(DB, Ls, KV_RANK), krope.reshape(DB, Ls, D_ROPE))):
            s_out[j].append(t)

    return (y_prompt.reshape(B, L, D), y_sample.reshape(DB, Ls, D),
            *[jnp.stack(t) for t in p_out], s_ret_all, s_c_all, *[jnp.stack(t) for t in s_out])
```

```python
import functools

import numpy as np
import jax
import jax.numpy as jnp
from jax import lax
from jax.experimental import pallas as pl
from jax.experimental.pallas import tpu as pltpu

F32 = jnp.float32
BF16 = jnp.bfloat16

HEAD_DIM = 128
H_RET = 4
H_ML = 4
H_MLA = 8
D_RET = H_RET * HEAD_DIM
D_ML = H_ML * HEAD_DIM
D_MLA = H_MLA * HEAD_DIM
Q_RANK = 512
KV_RANK = 256
D_NOPE = 128
D_ROPE = 64
ROPE_BASE = 10000.0
MIX_CHUNK = 128
NORM_EPS = 1e-6
MLSTM_EPS = 1e-6
ATT_SCALE = (D_NOPE + D_ROPE) ** -0.5
QK_SCALE = HEAD_DIM ** -0.5
IN_WIDTHS = (D_RET, D_RET, D_RET, D_RET, D_ML, D_ML, D_ML, D_ML, D_ML, H_ML, H_ML,
             Q_RANK, KV_RANK, D_ROPE, D_MLA)
D_IN = sum(IN_WIDTHS)

LANES = 128
VMEM_LIMIT_BYTES = 56 * 1024 * 1024

COL_RET = 0
COL_ML = 16
COL_CQ = 36
COL_CKV = 40
COL_AZ = 42
COL_GATE = 50
GATE_I_LANE = D_ROPE
GATE_F_LANE = D_ROPE + H_ML
N_IN_PAD = 52 * LANES
RET_LOG_G = [float(np.log1p(-np.exp2(np.float32(-5.0 - h)))) for h in range(H_RET)]


def _cparams(*sem):
    return pltpu.CompilerParams(dimension_semantics=sem, vmem_limit_bytes=VMEM_LIMIT_BYTES)


def _dot(a, b):
    return jnp.dot(a, b, preferred_element_type=F32)


def _dot_nt(a, b):
    return lax.dot_general(a, b, (((1,), (1,)), ((), ())), preferred_element_type=F32)


def _silu(x):
    return x * jax.nn.sigmoid(x)


def _rms(x, g):
    return x * lax.rsqrt(jnp.mean(x * x, axis=-1, keepdims=True) + NORM_EPS) * g


def _head_norm(o, g):
    mu = jnp.mean(o, axis=-1, keepdims=True)
    d = o - mu
    var = jnp.mean(d * d, axis=-1, keepdims=True)
    return d * lax.rsqrt(var + NORM_EPS) * g


def _rope128(x, cos, sin_signed):
    return x * cos + pltpu.roll(x, HEAD_DIM // 2, axis=1) * sin_signed


def _rope128b(x, cos, sin_signed):
    return x * cos + pltpu.roll(x, HEAD_DIM // 2, axis=2) * sin_signed


def _rope64(x, tab):
    q = D_ROPE // 2
    return x * tab[0] + pltpu.roll(x, LANES - q, axis=1) * tab[1] + pltpu.roll(x, q, axis=1) * tab[2]


def _rmsnorm_kernel(x_ref, g_ref, o_ref):
    o_ref[...] = _rms(x_ref[...], g_ref[...]).astype(o_ref.dtype)


def _rmsnorm(x, g, dtype):
    T, D = x.shape
    tm = min(T, 512)
    return pl.pallas_call(
        _rmsnorm_kernel, grid=(T // tm,),
        in_specs=[pl.BlockSpec((tm, D), lambda i: (i, 0)), pl.BlockSpec((1, D), lambda i: (0, 0))],
        out_specs=pl.BlockSpec((tm, D), lambda i: (i, 0)),
        out_shape=jax.ShapeDtypeStruct((T, D), dtype),
        compiler_params=_cparams("parallel"), name="rmsnorm")(x, g.reshape(1, D))


def _matmul_nt_kernel(h_ref, wt_ref, o_ref):
    o_ref[...] = _dot_nt(h_ref[...], wt_ref[...])


def _in_proj(hn, wt, layer, N):
    T, D = hn.shape
    tm = min(T, 2048)
    tn = 4 * LANES
    return pl.pallas_call(
        _matmul_nt_kernel, grid=(T // tm, N // tn),
        in_specs=[pl.BlockSpec((tm, D), lambda i, j: (i, 0)),
                  pl.BlockSpec((None, tn, D), lambda i, j: (layer, j, 0))],
        out_specs=pl.BlockSpec((tm, tn), lambda i, j: (i, j)),
        out_shape=jax.ShapeDtypeStruct((T, N), F32),
        compiler_params=_cparams("parallel", "arbitrary"), name="in_proj")(hn, wt)


def _ret_prompt_kernel(q_ref, k_ref, v_ref, z_ref, cs_ref, g_ref, o_ref, s_ref, S_scr):
    ci = pl.program_id(0)
    B, c = q_ref.shape[0], q_ref.shape[1]

    @pl.when(ci == 0)
    def _():
        S_scr[...] = jnp.zeros_like(S_scr)

    cos = cs_ref[0]
    sin = cs_ref[1]
    row = lax.broadcasted_iota(jnp.int32, (c, c), 0)
    col = lax.broadcasted_iota(jnp.int32, (c, c), 1)
    rel = (row - col).astype(F32)
    idx = lax.broadcasted_iota(jnp.int32, (c, 1), 0).astype(F32)
    for h in range(H_RET):
        sl = slice(h * HEAD_DIM, (h + 1) * HEAD_DIM)
        lg = RET_LOG_G[h]
        decay = jnp.where(rel >= 0, jnp.exp(lg * jnp.maximum(rel, 0.0)), 0.0)
        q_dec = jnp.exp(lg * (idx + 1.0))
        k_dec = jnp.exp(lg * (c - 1.0 - idx))
        bmm = lambda e, x, y: jnp.einsum(e, x, y, preferred_element_type=F32)
        q = _rope128b(q_ref[:, :, sl], cos, sin)
        k = _rope128b(k_ref[:, :, sl], cos, sin) * QK_SCALE
        vb = v_ref[:, :, sl].astype(BF16)
        qb = q.astype(BF16)
        S = S_scr[:, h]
        sc = bmm('bik,bjk->bij', qb, k.astype(BF16)) * decay
        o = bmm('bij,bjv->biv', sc.astype(BF16), vb) + bmm('bik,bkv->biv', qb, S.astype(BF16)) * q_dec
        kdt = jnp.swapaxes(k * k_dec, 1, 2).astype(BF16)
        S_scr[:, h] = S * float(np.exp(lg * c)) + bmm('bkj,bjv->bkv', kdt, vb)
        y = _head_norm(o, g_ref[:, sl]) * _silu(z_ref[:, :, sl])
        o_ref[:, :, sl] = y.astype(o_ref.dtype)

    @pl.when(ci == pl.num_programs(0) - 1)
    def _():
        s_ref[...] = S_scr[...]


def _ret_prompt(proj, cs, g, B, L):
    c = MIX_CHUNK if L % MIX_CHUNK == 0 else L
    wb = D_RET // LANES
    col = lambda j: (lambda i: (0, i, COL_RET // wb + j))
    proj3 = proj.reshape(B, L, proj.shape[1])
    o, st = pl.pallas_call(
        _ret_prompt_kernel, grid=(L // c,),
        in_specs=[pl.BlockSpec((B, c, D_RET), col(0)), pl.BlockSpec((B, c, D_RET), col(1)),
                  pl.BlockSpec((B, c, D_RET), col(2)), pl.BlockSpec((B, c, D_RET), col(3)),
                  pl.BlockSpec((2, c, HEAD_DIM), lambda i: (0, i, 0)),
                  pl.BlockSpec((1, D_RET), lambda i: (0, 0))],
        out_specs=[pl.BlockSpec((B, c, D_RET), lambda i: (0, i, 0)),
                   pl.BlockSpec((B, H_RET, HEAD_DIM, HEAD_DIM), lambda i: (0, 0, 0, 0))],
        out_shape=[jax.ShapeDtypeStruct((B, L, D_RET), BF16),
                   jax.ShapeDtypeStruct((B, H_RET, HEAD_DIM, HEAD_DIM), F32)],
        scratch_shapes=[pltpu.VMEM((B, H_RET, HEAD_DIM, HEAD_DIM), F32)],
        compiler_params=_cparams("arbitrary"), name="ret_prompt")(proj3, proj3, proj3, proj3, cs, g)
    return o.reshape(B * L, D_RET), st


def _log_sigmoid(x):
    return -(jnp.maximum(-x, 0.0) + jnp.log1p(jnp.exp(-jnp.abs(x))))


def _mlstm_prompt_kernel(q_ref, k_ref, v_ref, og_ref, z_ref, gate_ref, bias_ref, g_ref,
                         o_ref, c_ref, n_ref, m_ref, C_scr, M_scr):
    ci = pl.program_id(0)
    B, c = q_ref.shape[0], q_ref.shape[1]

    @pl.when(ci == 0)
    def _():
        C_scr[...] = jnp.zeros_like(C_scr)
        M_scr[...] = jnp.zeros_like(M_scr)

    row = lax.broadcasted_iota(jnp.int32, (c, c), 0)
    col = lax.broadcasted_iota(jnp.int32, (c, c), 1)
    causal = col <= row
    eye = col == row
    ones = jnp.ones((B, c, HEAD_DIM), BF16)
    gates = gate_ref[...] + bias_ref[...]
    bmm = lambda e, x, y: jnp.einsum(e, x, y, preferred_element_type=F32)
    for h in range(H_ML):
        sl = slice(h * HEAD_DIM, (h + 1) * HEAD_DIM)
        q = q_ref[:, :, sl]
        k = k_ref[:, :, sl] * QK_SCALE
        vb1 = jnp.concatenate([v_ref[:, :, sl].astype(BF16), ones], axis=2)
        i_col = gates[:, :, GATE_I_LANE + h:GATE_I_LANE + h + 1]
        lf_col = _log_sigmoid(gates[:, :, GATE_F_LANE + h:GATE_F_LANE + h + 1])
        lf_row = jnp.sum(jnp.where(eye, lf_col, 0.0), axis=1, keepdims=True)
        i_row = jnp.sum(jnp.where(eye, i_col, 0.0), axis=1, keepdims=True)
        b_col = jnp.sum(jnp.where(causal, lf_row, 0.0), axis=2, keepdims=True)
        b_row = jnp.sum(jnp.where(row <= col, lf_col, 0.0), axis=1, keepdims=True)
        m_prev = M_scr[:, h][:, :, 0:1]
        log_d = jnp.where(causal, b_col - b_row + i_row, -jnp.inf)
        inter = b_col + m_prev
        m_t = jnp.maximum(inter, jnp.max(log_d, axis=2, keepdims=True))
        w_intra = jnp.exp(log_d - m_t)
        w_inter = jnp.exp(inter - m_t)
        qb = q.astype(BF16)
        C1 = C_scr[:, h]
        s = bmm('bik,bjk->bij', qb, k.astype(BF16)) * w_intra
        a = bmm('bij,bjv->biv', s.astype(BF16), vb1) + w_inter * bmm('bik,bkv->biv', qb, C1.astype(BF16))
        hb = a[:, :, :HEAD_DIM] / (jnp.maximum(jnp.abs(a[:, :, HEAD_DIM:]), jnp.exp(-m_t)) + MLSTM_EPS)
        m_new = m_t[:, c - 1:c, :]
        b_last = b_col[:, c - 1:c, :]
        w_k = jnp.exp(b_last - b_col + i_col - m_new)
        dec = jnp.exp(b_last + m_prev - m_new)
        kwt = jnp.swapaxes(k * w_k, 1, 2).astype(BF16)
        C_scr[:, h] = dec * C1 + bmm('bkj,bjv->bkv', kwt, vb1)
        M_scr[:, h] = jnp.broadcast_to(m_new, (B, 1, HEAD_DIM))
        y = _head_norm(hb * jax.nn.sigmoid(og_ref[:, :, sl]), g_ref[:, sl]) * _silu(z_ref[:, :, sl])
        o_ref[:, :, sl] = y.astype(o_ref.dtype)

    @pl.when(ci == pl.num_programs(0) - 1)
    def _():
        m_ref[...] = M_scr[...]
        for b in range(B):
            for h in range(H_ML):
                C1 = C_scr[b, h]
                c_ref[b, h] = C1[:, :HEAD_DIM]
                n_ref[b, h] = C1[:, HEAD_DIM:].T[0:1, :]


def _mlstm_prompt(proj, proj_tail, gate_bias, g, B, L):
    c = MIX_CHUNK if L % MIX_CHUNK == 0 else L
    wb = D_ML // LANES
    col = lambda j: (lambda i: (0, i, COL_ML // wb + j))
    proj3 = proj.reshape(B, L, proj.shape[1])
    tail3 = proj_tail.reshape(B, L, proj_tail.shape[1])
    vec = jax.ShapeDtypeStruct((B, H_ML, 1, HEAD_DIM), F32)
    vec_spec = pl.BlockSpec((B, H_ML, 1, HEAD_DIM), lambda i: (0, 0, 0, 0))
    o, st_c, st_n, st_m = pl.pallas_call(
        _mlstm_prompt_kernel, grid=(L // c,),
        in_specs=[pl.BlockSpec((B, c, D_ML), col(0)), pl.BlockSpec((B, c, D_ML), col(1)),
                  pl.BlockSpec((B, c, D_ML), col(2)), pl.BlockSpec((B, c, D_ML), col(3)),
                  pl.BlockSpec((B, c, D_ML), col(4)),
                  pl.BlockSpec((B, c, LANES), lambda i: (0, i, COL_GATE)),
                  pl.BlockSpec((1, LANES), lambda i: (0, 0)),
                  pl.BlockSpec((1, D_ML), lambda i: (0, 0))],
        out_specs=[pl.BlockSpec((B, c, D_ML), lambda i: (0, i, 0)),
                   pl.BlockSpec((B, H_ML, HEAD_DIM, HEAD_DIM), lambda i: (0, 0, 0, 0)),
                   vec_spec, vec_spec],
        out_shape=[jax.ShapeDtypeStruct((B, L, D_ML), BF16),
                   jax.ShapeDtypeStruct((B, H_ML, HEAD_DIM, HEAD_DIM), F32), vec, vec],
        scratch_shapes=[pltpu.VMEM((B, H_ML, HEAD_DIM, 2 * HEAD_DIM), F32),
                        pltpu.VMEM((B, H_ML, 1, HEAD_DIM), F32)],
        compiler_params=_cparams("arbitrary"),
        name="mlstm_prompt")(proj3, proj3, proj3, proj3, proj3, tail3, gate_bias, g)
    return o.reshape(B * L, D_ML), st_c, st_n, st_m


STEP_TOKENS = 16


def _outer_t(kb, vm):
    return lax.dot_general(kb, vm, (((0,), (0,)), ((), ())), preferred_element_type=F32)


def _ret_step_kernel(q_ref, k_ref, v_ref, z_ref, cs_ref, g_ref, s_ref, *rest):
    o_ref, so_ref = rest[-2:]
    tb = q_ref.shape[0]
    cos = cs_ref[0]
    sin = cs_ref[1]
    tok = lax.broadcasted_iota(jnp.int32, (tb, HEAD_DIM), 0)
    for h in range(H_RET):
        sl = slice(h * HEAD_DIM, (h + 1) * HEAD_DIM)
        gamma = float(np.exp(RET_LOG_G[h]))
        q = _rope128(q_ref[:, sl], cos, sin)
        k = _rope128(k_ref[:, sl], cos, sin) * QK_SCALE
        v = v_ref[:, sl]
        qb = q.astype(BF16)
        kb = k.astype(BF16)
        ro = jnp.zeros((tb, HEAD_DIM), F32)
        for t in range(tb):
            S = s_ref[0, t, h]
            ro = jnp.where(tok == t, _dot(qb, S.astype(BF16)), ro)
            vm = jnp.where(tok == t, v, 0.0).astype(BF16)
            so_ref[0, t, h] = gamma * S + _outer_t(kb, vm)
        o = jnp.sum(q * k, axis=1, keepdims=True) * v + gamma * ro
        y = _head_norm(o, g_ref[:, sl]) * _silu(z_ref[:, sl])
        o_ref[:, sl] = y.astype(o_ref.dtype)


def _stacked_state_specs(prev, depth, T, tb, H, layer):
    shape = jax.ShapeDtypeStruct((depth, T, H, HEAD_DIM, HEAD_DIM), F32)
    spec = pl.BlockSpec((1, tb, H, HEAD_DIM, HEAD_DIM), lambda i: (layer, i, 0, 0, 0))
    if prev is None:
        prev = jnp.zeros(shape.shape, shape.dtype)
    return shape, spec, [pl.BlockSpec(memory_space=pl.ANY)], [prev]


def _ret_step(proj, cs, g, state, prev, layer):
    T = proj.shape[0]
    depth = state.shape[0]
    tb = min(T, STEP_TOKENS)
    wb = D_RET // LANES
    col = lambda j: (lambda i: (i, COL_RET // wb + j))
    st_shape, st_spec, extra_specs, extra_args = _stacked_state_specs(prev, depth, T, tb, H_RET, layer)
    n_in = 7
    return pl.pallas_call(
        _ret_step_kernel, grid=(T // tb,),
        in_specs=[pl.BlockSpec((tb, D_RET), col(0)), pl.BlockSpec((tb, D_RET), col(1)),
                  pl.BlockSpec((tb, D_RET), col(2)), pl.BlockSpec((tb, D_RET), col(3)),
                  pl.BlockSpec((2, tb, HEAD_DIM), lambda i: (0, i, 0)),
                  pl.BlockSpec((1, D_RET), lambda i: (0, 0)),
                  pl.BlockSpec((1, tb, H_RET, HEAD_DIM, HEAD_DIM), lambda i: (layer, i, 0, 0, 0))] + extra_specs,
        out_specs=[pl.BlockSpec((tb, D_RET), lambda i: (i, 0)), st_spec],
        out_shape=[jax.ShapeDtypeStruct((T, D_RET), BF16), st_shape],
        input_output_aliases={n_in: 1},
        compiler_params=_cparams("parallel"), name="ret_step")(proj, proj, proj, proj, cs, g, state, *extra_args)


def _mlstm_step_kernel(q_ref, k_ref, v_ref, og_ref, z_ref, gate_ref, bias_ref, g_ref, c_ref, n_ref, m_ref,
                       *rest):
    o_ref, co_ref, no_ref, mo_ref = rest[-4:]
    tb = q_ref.shape[0]
    gates = gate_ref[...] + bias_ref[...]
    m_all = m_ref[0]
    tok = lax.broadcasted_iota(jnp.int32, (tb, HEAD_DIM), 0)
    for h in range(H_ML):
        sl = slice(h * HEAD_DIM, (h + 1) * HEAD_DIM)
        q = q_ref[:, sl]
        k = k_ref[:, sl] * QK_SCALE
        v = v_ref[:, sl]
        ib = gates[:, GATE_I_LANE + h:GATE_I_LANE + h + 1]
        lf = _log_sigmoid(gates[:, GATE_F_LANE + h:GATE_F_LANE + h + 1])
        inter = lf + m_all[:, h:h + 1]
        m_t = jnp.maximum(inter, ib)
        w_intra = jnp.exp(ib - m_t)
        w_inter = jnp.exp(inter - m_t)
        nv = n_ref[0, :, h, :]
        qb = q.astype(BF16)
        kb = k.astype(BF16)
        vw = v * w_intra
        ro = jnp.zeros((tb, HEAD_DIM), F32)
        for t in range(tb):
            Cm = c_ref[0, t, h]
            ro = jnp.where(tok == t, _dot(qb, Cm.astype(BF16)), ro)
            vm = jnp.where(tok == t, vw, 0.0).astype(BF16)
            co_ref[0, t, h] = w_inter[t:t + 1, :] * Cm + _outer_t(kb, vm)
        s = jnp.sum(q * k, axis=1, keepdims=True) * w_intra
        num = s * v + w_inter * ro
        qn = s + w_inter * jnp.sum(q * nv, axis=1, keepdims=True)
        hb = num / (jnp.maximum(jnp.abs(qn), jnp.exp(-m_t)) + MLSTM_EPS)
        no_ref[:, h, :] = w_inter * nv + w_intra * k
        mo_ref[:, sl] = jnp.broadcast_to(m_t, (tb, HEAD_DIM))
        y = _head_norm(hb * jax.nn.sigmoid(og_ref[:, sl]), g_ref[:, sl]) * _silu(z_ref[:, sl])
        o_ref[:, sl] = y.astype(o_ref.dtype)


def _mlstm_step(proj, proj_tail, gate_bias, g, c_state, n_state, m_state, prev, layer):
    T = proj.shape[0]
    depth = c_state.shape[0]
    tb = min(T, STEP_TOKENS)
    wb = D_ML // LANES
    col = lambda j: (lambda i: (i, COL_ML // wb + j))
    st_shape, st_spec, extra_specs, extra_args = _stacked_state_specs(prev, depth, T, tb, H_ML, layer)
    n_in = 11
    return pl.pallas_call(
        _mlstm_step_kernel, grid=(T // tb,),
        in_specs=[pl.BlockSpec((tb, D_ML), col(0)), pl.BlockSpec((tb, D_ML), col(1)),
                  pl.BlockSpec((tb, D_ML), col(2)), pl.BlockSpec((tb, D_ML), col(3)),
                  pl.BlockSpec((tb, D_ML), col(4)),
                  pl.BlockSpec((tb, LANES), lambda i: (i, COL_GATE)),
                  pl.BlockSpec((1, LANES), lambda i: (0, 0)),
                  pl.BlockSpec((1, D_ML), lambda i: (0, 0)),
                  pl.BlockSpec((1, tb, H_ML, HEAD_DIM, HEAD_DIM), lambda i: (layer, i, 0, 0, 0)),
                  pl.BlockSpec((1, tb, H_ML, HEAD_DIM), lambda i: (layer, i, 0, 0)),
                  pl.BlockSpec((1, tb, H_ML), lambda i: (layer, i, 0))] + extra_specs,
        out_specs=[pl.BlockSpec((tb, D_ML), lambda i: (i, 0)), st_spec,
                   pl.BlockSpec((tb, H_ML, HEAD_DIM), lambda i: (i, 0, 0)),
                   pl.BlockSpec((tb, D_ML), lambda i: (i, 0))],
        out_shape=[jax.ShapeDtypeStruct((T, D_ML), BF16), st_shape,
                   jax.ShapeDtypeStruct((T, H_ML, HEAD_DIM), F32),
                   jax.ShapeDtypeStruct((T, D_ML), F32)],
        input_output_aliases={n_in: 1},
        compiler_params=_cparams("parallel"),
        name="mlstm_step")(proj, proj, proj, proj, proj, proj_tail, gate_bias, g, c_state, n_state, m_state,
                           *extra_args)


def _mla_common(cq_ref, ckv_ref, gate_ref, tab_ref, gq_ref, gkv_ref, wqn_ref, wqr_ref, ckvn_ref, kr_ref):
    tab = tab_ref[...]
    cqn = _rms(cq_ref[...], gq_ref[...]).astype(BF16)
    ckvn = _rms(ckv_ref[...], gkv_ref[...])
    ckvn_ref[...] = ckvn
    k_rope = _rope64(gate_ref[...], tab)
    kr_ref[...] = k_rope[:, :D_ROPE]
    q_nope = _dot_nt(cqn, wqn_ref[...])
    q_rope = _dot_nt(cqn, wqr_ref[...])
    return tab, ckvn, k_rope, q_nope, q_rope


def _mla_prep_prompt_kernel(cq_ref, ckv_ref, gate_ref, tab_ref, gq_ref, gkv_ref, wqn_ref, wqr_ref, wk_ref, wv_ref,
                            ckvn_ref, kr_ref, q_ref, k_ref, v_ref):
    tab, ckvn, k_rope, q_nope, q_rope = _mla_common(cq_ref, ckv_ref, gate_ref, tab_ref, gq_ref, gkv_ref,
                                                    wqn_ref, wqr_ref, ckvn_ref, kr_ref)
    cb = ckvn.astype(BF16)
    k_nope = _dot(cb, wk_ref[...])
    v_ref[...] = _dot(cb, wv_ref[...]).astype(v_ref.dtype)
    krb = k_rope.astype(k_ref.dtype)
    for h in range(H_MLA):
        sl = slice(h * HEAD_DIM, (h + 1) * HEAD_DIM)
        lo = slice(2 * h * LANES, (2 * h + 1) * LANES)
        hi = slice((2 * h + 1) * LANES, (2 * h + 2) * LANES)
        q_ref[:, lo] = (q_nope[:, sl] * ATT_SCALE).astype(q_ref.dtype)
        q_ref[:, hi] = (_rope64(q_rope[:, sl], tab) * ATT_SCALE).astype(q_ref.dtype)
        k_ref[:, lo] = k_nope[:, sl].astype(k_ref.dtype)
        k_ref[:, hi] = krb


def _mla_prep_sample_kernel(cq_ref, ckv_ref, gate_ref, tab_ref, gq_ref, gkv_ref, wqn_ref, wqr_ref, wukt_ref,
                            ckvn_ref, kr_ref, qs_ref):
    tab, ckvn, k_rope, q_nope, q_rope = _mla_common(cq_ref, ckv_ref, gate_ref, tab_ref, gq_ref, gkv_ref,
                                                    wqn_ref, wqr_ref, ckvn_ref, kr_ref)
    for h in range(H_MLA):
        sl = slice(h * HEAD_DIM, (h + 1) * HEAD_DIM)
        q_lat = _dot(q_nope[:, sl].astype(BF16), wukt_ref[h])
        qs_ref[h, :, 0:KV_RANK] = q_lat * ATT_SCALE
        qs_ref[h, :, KV_RANK:KV_RANK + LANES] = _rope64(q_rope[:, sl], tab) * ATT_SCALE


def _mla_prep(proj, tab, gq, gkv, wqn, wqr, extra_w, layer, sample):
    T = proj.shape[0]
    tm = min(T, 512)
    tab_blocks = tab.shape[1] // tm
    full = lambda a: pl.BlockSpec(a.shape, lambda i: (0,) * a.ndim)
    of_layer = lambda a: pl.BlockSpec((None,) + a.shape[1:], lambda i: (layer,) + (0,) * (a.ndim - 1))
    in_specs = [pl.BlockSpec((tm, Q_RANK), lambda i: (i, COL_CQ * LANES // Q_RANK)),
                pl.BlockSpec((tm, KV_RANK), lambda i: (i, COL_CKV * LANES // KV_RANK)),
                pl.BlockSpec((tm, LANES), lambda i: (i, COL_GATE)),
                pl.BlockSpec((3, tm, LANES), lambda i: (0, i % tab_blocks, 0)),
                full(gq), full(gkv), of_layer(wqn), of_layer(wqr)] + [of_layer(w) for w in extra_w]
    out_specs = [pl.BlockSpec((tm, KV_RANK), lambda i: (i, 0)), pl.BlockSpec((tm, D_ROPE), lambda i: (i, 0))]
    out_shape = [jax.ShapeDtypeStruct((T, KV_RANK), F32), jax.ShapeDtypeStruct((T, D_ROPE), F32)]
    if sample:
        kern = _mla_prep_sample_kernel
        out_specs.append(pl.BlockSpec((H_MLA, tm, KV_RANK + LANES), lambda i: (0, i, 0)))
        out_shape.append(jax.ShapeDtypeStruct((H_MLA, T, KV_RANK + LANES), F32))
    else:
        kern = _mla_prep_prompt_kernel
        out_specs += [pl.BlockSpec((tm, 2 * D_MLA), lambda i: (i, 0)), pl.BlockSpec((tm, 2 * D_MLA), lambda i: (i, 0)),
                      pl.BlockSpec((tm, D_MLA), lambda i: (i, 0))]
        out_shape += [jax.ShapeDtypeStruct((T, 2 * D_MLA), BF16), jax.ShapeDtypeStruct((T, 2 * D_MLA), BF16),
                      jax.ShapeDtypeStruct((T, D_MLA), BF16)]
    return pl.pallas_call(
        kern, grid=(T // tm,), in_specs=in_specs, out_specs=out_specs, out_shape=out_shape,
        compiler_params=_cparams("parallel"),
        name="mla_prep_sample" if sample else "mla_prep_prompt")(proj, proj, proj, tab, gq, gkv, wqn, wqr, *extra_w)


def _attn_prompt_kernel(q_ref, k_ref, v_ref, z_ref, o_ref, *, tq, nh):
    L = q_ref.shape[0]
    n = L // tq
    row = lax.broadcasted_iota(jnp.int32, (tq, tq), 0)
    col = lax.broadcasted_iota(jnp.int32, (tq, tq), 1)
    tiles = [(h, i) for h in range(nh) for i in range(n)]

    def scores(t):
        h, i = t
        cur = slice(i * tq, (i + 1) * tq)
        hq = slice(h * 2 * LANES, (h + 1) * 2 * LANES)
        q = q_ref[cur, hq]
        sd = _dot_nt(q, k_ref[cur, hq])
        so = _dot_nt(q, k_ref[0:i * tq, hq]) if i > 0 else None
        return sd, so

    def values(t, pd, po, l):
        h, i = t
        cur = slice(i * tq, (i + 1) * tq)
        hv = slice(h * HEAD_DIM, (h + 1) * HEAD_DIM)
        acc = _dot(pd, v_ref[cur, hv])
        if i > 0:
            acc = acc + _dot(po, v_ref[0:i * tq, hv])
        o_ref[cur, hv] = (acc / l * _silu(z_ref[cur, hv])).astype(o_ref.dtype)

    nxt = scores(tiles[0])
    prev = None
    for idx, t in enumerate(tiles):
        i = t[1]
        sd, so = nxt
        if idx + 1 < len(tiles):
            nxt = scores(tiles[idx + 1])
        if prev is not None:
            values(tiles[idx - 1], *prev)
        sd = jnp.where(col <= row, sd, -jnp.inf)
        m = jnp.max(sd, axis=1, keepdims=True)
        if i > 0:
            m = jnp.maximum(m, jnp.max(so, axis=1, keepdims=True))
        pd = jnp.exp(sd - m)
        l = jnp.sum(pd, axis=1, keepdims=True)
        po = None
        if i > 0:
            po = jnp.exp(so - m)
            l = l + jnp.sum(po, axis=1, keepdims=True)
            po = po.astype(BF16)
        prev = (pd.astype(BF16), po, l)
    values(tiles[-1], *prev)


ATTN_HEADS_PER_STEP = 2


def _attn_prompt(q, k, v, proj, B, L):
    tq = min(L, 256)
    nh = ATTN_HEADS_PER_STEP
    assert H_MLA % nh == 0 and COL_AZ % nh == 0
    return pl.pallas_call(
        functools.partial(_attn_prompt_kernel, tq=tq, nh=nh), grid=(B, H_MLA // nh),
        in_specs=[pl.BlockSpec((L, nh * 2 * LANES), lambda b, h: (b, h)),
                  pl.BlockSpec((L, nh * 2 * LANES), lambda b, h: (b, h)),
                  pl.BlockSpec((L, nh * HEAD_DIM), lambda b, h: (b, h)),
                  pl.BlockSpec((L, nh * HEAD_DIM), lambda b, h: (b, COL_AZ // nh + h))],
        out_specs=pl.BlockSpec((L, nh * HEAD_DIM), lambda b, h: (b, h)),
        out_shape=jax.ShapeDtypeStruct((B * L, D_MLA), BF16),
        compiler_params=_cparams("parallel", "parallel"), name="attn_prompt")(q, k, v, proj)


DECODE_SLOTS = 3


def _decode_kernel(pt_ref, qs_ref, cn_ref, kn_ref, ckv_hbm, krt_hbm, o_ref, ckv_buf, krt_buf, sem,
                   *, layer, n_pages, page, n_seq):
    i = pl.program_id(0)

    def page_copies(seq, slot):
        out = []
        for p in range(n_pages):
            pg = pt_ref[seq, p]
            rows = pl.ds(p * page, page)
            out.append(pltpu.make_async_copy(ckv_hbm.at[layer, pg], ckv_buf.at[slot, rows, :], sem.at[0, slot]))
            out.append(pltpu.make_async_copy(krt_hbm.at[layer, pg], krt_buf.at[slot, :, rows], sem.at[1, slot]))
        return out

    def start_fetch(seq, slot):
        for cp in page_copies(seq, slot):
            cp.start()

    def wait_fetch(seq, slot):
        for cp in page_copies(seq, slot):
            cp.wait()

    ahead = DECODE_SLOTS - 1

    @pl.when(i == 0)
    def _():
        for j in range(min(ahead, n_seq)):
            start_fetch(j, j)

    @pl.when(i + ahead < n_seq)
    def _():
        start_fetch(i + ahead, lax.rem(i + ahead, DECODE_SLOTS))

    slot = lax.rem(i, DECODE_SLOTS)
    wait_fetch(i, slot)

    q = qs_ref[0]
    q16 = jnp.concatenate([q, jnp.zeros_like(q)], axis=0).astype(BF16)
    ql = q16[:, 0:KV_RANK]
    qr = q16[:, KV_RANK:KV_RANK + D_ROPE]
    kc = ckv_buf[slot].astype(BF16)
    sc = _dot_nt(ql, kc) + _dot(qr, krt_buf[slot].astype(BF16))
    cn = cn_ref[0].astype(BF16).astype(F32)
    kn = kn_ref[0].astype(BF16).astype(F32)
    s_new = (jnp.sum(ql.astype(F32) * cn, axis=1, keepdims=True)
             + jnp.sum(qr.astype(F32) * kn, axis=1, keepdims=True))
    m = jnp.maximum(jnp.max(sc, axis=1, keepdims=True), s_new)
    p = jnp.exp(sc - m)
    p_new = jnp.exp(s_new - m)
    l = jnp.sum(p, axis=1, keepdims=True) + p_new
    pb = p.astype(BF16)
    half = kc.shape[0] // 2
    acc = (_dot(pb[:, :half], kc[:half]) + _dot(pb[:, half:], kc[half:])
           + p_new.astype(BF16).astype(F32) * cn)
    o_ref[0] = (acc / l)[0:H_MLA, :]


def _decode(page_table, qs, ckv_new, kr_new, cache_ckv, cache_krope_t, layer):
    DB, n_pages = page_table.shape
    page = cache_ckv.shape[2]
    P = n_pages * page
    kern = functools.partial(_decode_kernel, layer=layer, n_pages=n_pages, page=page, n_seq=DB)
    grid_spec = pltpu.PrefetchScalarGridSpec(
        num_scalar_prefetch=1, grid=(DB,),
        in_specs=[pl.BlockSpec((1, H_MLA, KV_RANK + LANES), lambda b, pt: (b, 0, 0)),
                  pl.BlockSpec((1, 1, KV_RANK), lambda b, pt: (b, 0, 0)),
                  pl.BlockSpec((1, 1, D_ROPE), lambda b, pt: (b, 0, 0)),
                  pl.BlockSpec(memory_space=pl.ANY), pl.BlockSpec(memory_space=pl.ANY)],
        out_specs=pl.BlockSpec((1, H_MLA, KV_RANK), lambda b, pt: (b, 0, 0)),
        scratch_shapes=[pltpu.VMEM((DECODE_SLOTS, P, KV_RANK), F32), pltpu.VMEM((DECODE_SLOTS, D_ROPE, P), F32),
                        pltpu.SemaphoreType.DMA((2, DECODE_SLOTS))])
    return pl.pallas_call(
        kern, grid_spec=grid_spec, out_shape=jax.ShapeDtypeStruct((DB, H_MLA, KV_RANK), F32),
        compiler_params=_cparams("arbitrary"), name="mla_decode")(
            page_table, qs, ckv_new.reshape(DB, 1, KV_RANK), kr_new.reshape(DB, 1, D_ROPE), cache_ckv, cache_krope_t)


def _mla_out_kernel(ol_ref, w_ref, z_ref, o_ref):
    o = _dot(ol_ref[0].astype(BF16), w_ref[...])
    o_ref[...] = (o * _silu(z_ref[...])).astype(o_ref.dtype)


def _mla_out(o_lat_h, wuv, layer, proj):
    T = proj.shape[0]
    return pl.pallas_call(
        _mla_out_kernel, grid=(H_MLA,),
        in_specs=[pl.BlockSpec((1, T, KV_RANK), lambda h: (h, 0, 0)),
                  pl.BlockSpec((None, None, KV_RANK, HEAD_DIM), lambda h: (layer, h, 0, 0)),
                  pl.BlockSpec((T, HEAD_DIM), lambda h: (0, COL_AZ + h))],
        out_specs=pl.BlockSpec((T, HEAD_DIM), lambda h: (0, h)),
        out_shape=jax.ShapeDtypeStruct((T, D_MLA), BF16),
        compiler_params=_cparams("parallel"), name="mla_out")(o_lat_h, wuv, proj)


def _out_proj_kernel(or_ref, om_ref, oa_ref, x_ref, w_ref, g_ref, *out_refs, last):
    x = (x_ref[...] + _dot(or_ref[...], w_ref[0:D_RET, :]) + _dot(om_ref[...], w_ref[D_RET:D_RET + D_ML, :])
         + _dot(oa_ref[...], w_ref[D_RET + D_ML:, :]))
    hn = _rms(x, g_ref[...])
    if last:
        out_refs[0][...] = hn
    else:
        out_refs[0][...] = x
        out_refs[1][...] = hn.astype(out_refs[1].dtype)


def _out_proj(o_ret, o_ml, o_mla, x, w, layer, g, last):
    T, D = x.shape
    tm = min(T, 512)
    row = lambda n: pl.BlockSpec((tm, n), lambda i: (i, 0))
    if last:
        out_specs, out_shape = [row(D)], [jax.ShapeDtypeStruct((T, D), F32)]
    else:
        out_specs = [row(D), row(D)]
        out_shape = [jax.ShapeDtypeStruct((T, D), F32), jax.ShapeDtypeStruct((T, D), BF16)]
    return pl.pallas_call(
        functools.partial(_out_proj_kernel, last=last), grid=(T // tm,),
        in_specs=[row(D_RET), row(D_ML), row(D_MLA), row(D),
                  pl.BlockSpec((None,) + w.shape[1:], lambda i: (layer, 0, 0)),
                  pl.BlockSpec((1, D), lambda i: (0, 0))],
        out_specs=out_specs, out_shape=out_shape,
        compiler_params=_cparams("parallel"), name="out_proj")(o_ret, o_ml, o_mla, x, w, g.reshape(1, D))


def _rope_tables(pos):
    pos = pos.astype(F32)[:, None]

    def cs(half):
        inv = ROPE_BASE ** (-jnp.arange(half, dtype=F32) / half)
        ang = pos * inv[None, :]
        return jnp.cos(ang), jnp.sin(ang)

    c64, s64 = cs(HEAD_DIM // 2)
    tab128 = jnp.stack([jnp.concatenate([c64, c64], -1), jnp.concatenate([-s64, s64], -1)])
    c32, s32 = cs(D_ROPE // 2)
    z32 = jnp.zeros_like(c32)
    tab64 = jnp.stack([jnp.concatenate([c32, c32, z32, z32], -1),
                       jnp.concatenate([-s32, z32, z32, z32], -1),
                       jnp.concatenate([z32, s32, z32, z32], -1)])
    return tab128, tab64


RELAYOUT_ROWS = 4 * LANES


def _w_in_block_plan():
    o = np.cumsum((0,) + IN_WIDTHS)
    order = [(0, 9), (11, 12), (12, 13), (14, 15), (13, 14), (9, 11)]
    plan = [[] for _ in range(N_IN_PAD // RELAYOUT_ROWS)]
    dst = 0
    for a, b in order:
        src, rows = int(o[a]), int(o[b] - o[a])
        while rows:
            blk, off = divmod(dst, RELAYOUT_ROWS)
            n = min(rows, RELAYOUT_ROWS - off)
            plan[blk].append((src, off, n))
            src, dst, rows = src + n, dst + n, rows - n
    return plan


def _relayout_kernel(wt_hbm, o_ref, buf, sem, *, plan):
    l = pl.program_id(0)
    j = pl.program_id(1)
    nb = len(plan)

    def copies(layer, blk, slot):
        return [pltpu.make_async_copy(wt_hbm.at[layer, pl.ds(src, n), :], buf.at[slot, pl.ds(off, n), :], sem.at[slot])
                for src, off, n in plan[blk]]

    g = l * nb + j
    slot = lax.rem(g, 2)

    @pl.when(g == 0)
    def _():
        for cp in copies(0, 0, 0):
            cp.start()

    for jj in range(nb):
        @pl.when(j == jj)
        def _(jj=jj):
            if jj + 1 < nb:
                for cp in copies(l, jj + 1, 1 - slot):
                    cp.start()
            else:
                @pl.when(l + 1 < pl.num_programs(0))
                def _():
                    for cp in copies(l + 1, 0, 1 - slot):
                        cp.start()
            for cp in copies(l, jj, slot):
                cp.wait()
            covered = sum(n for _, _, n in plan[jj])
            o_ref[0:covered, :] = buf[slot, 0:covered, :].astype(o_ref.dtype)
            if covered < RELAYOUT_ROWS:
                o_ref[covered:, :] = jnp.zeros((RELAYOUT_ROWS - covered, o_ref.shape[1]), o_ref.dtype)


def _layout_w_in(w_in):
    wt = jnp.swapaxes(w_in, 1, 2)
    depth, _, D = wt.shape
    plan = _w_in_block_plan()
    return pl.pallas_call(
        functools.partial(_relayout_kernel, plan=plan), grid=(depth, len(plan)),
        in_specs=[pl.BlockSpec(memory_space=pl.ANY)],
        out_specs=pl.BlockSpec((None, RELAYOUT_ROWS, D), lambda l, j: (l, j, 0)),
        out_shape=jax.ShapeDtypeStruct((depth, N_IN_PAD, D), BF16),
        scratch_shapes=[pltpu.VMEM((2, RELAYOUT_ROWS, D), F32), pltpu.SemaphoreType.DMA((2,))],
        compiler_params=_cparams("arbitrary", "arbitrary"), name="w_in_relayout")(wt)


def kernel(x_prompt, x_sample, state_ret, state_mlstm_c, state_mlstm_n, state_mlstm_m, cache_ckv, cache_krope,
           page_table, g_norm, w_in, b_ig, b_fg, g_ret, g_ml, g_q, g_kv, w_uq, w_uk, w_uv, w_out, g_final):
    B, L, D = x_prompt.shape
    DB, Ls, _ = x_sample.shape
    assert Ls == 1, "the sample group is one new token per sequence"
    depth = w_in.shape[0]
    past_len = page_table.shape[1] * cache_ckv.shape[2]

    tab128_p, tab64_p = _rope_tables(jnp.arange(L))
    tab128_s, tab64_s = _rope_tables(past_len + jnp.arange(Ls))
    tab128_s = jnp.broadcast_to(tab128_s, (2, DB, HEAD_DIM))
    tab64_s = jnp.broadcast_to(tab64_s, (3, DB, LANES))
    w_in_b = _layout_w_in(w_in)
    w_out_b = w_out.astype(BF16)
    w_uq_t = jnp.transpose(w_uq, (0, 2, 3, 1))
    wqn = w_uq_t[:, :, :D_NOPE, :].reshape(depth, D_MLA, Q_RANK).astype(BF16)
    wqr = jnp.concatenate([w_uq_t[:, :, D_NOPE:, :], jnp.zeros((depth, H_MLA, LANES - D_ROPE, Q_RANK), w_uq.dtype)],
                          2).reshape(depth, D_MLA, Q_RANK).astype(BF16)
    wk = w_uk.reshape(depth, KV_RANK, D_MLA).astype(BF16)
    wv = w_uv.reshape(depth, KV_RANK, D_MLA).astype(BF16)
    wukt = jnp.transpose(w_uk, (0, 2, 3, 1)).astype(BF16)
    wuv_h = jnp.transpose(w_uv, (0, 2, 1, 3)).astype(BF16)
    zeros_gate = jnp.zeros((depth, LANES - D_ROPE - 2 * H_ML), F32)
    gate_bias = jnp.concatenate([jnp.zeros((depth, D_ROPE), F32), b_ig, b_fg, zeros_gate], -1)

    cache_krope_t = jnp.swapaxes(cache_krope, 2, 3)

    xp = x_prompt.reshape(B * L, D)
    xs = x_sample.reshape(DB, D)
    hp = _rmsnorm(xp, g_norm[0], BF16)
    hs = _rmsnorm(xs, g_norm[0], BF16)
    p_out = [[] for _ in range(6)]
    s_out = [[] for _ in range(4)]
    s_ret_all = s_c_all = None
    for l in range(depth):
        last = l == depth - 1
        g_next = g_final if last else g_norm[l + 1]
        gq, gkv = g_q[l].reshape(1, Q_RANK), g_kv[l].reshape(1, KV_RANK)
        gr, gm = g_ret[l].reshape(1, D_RET), g_ml[l].reshape(1, D_ML)
        gb = gate_bias[l].reshape(1, LANES)

        proj = tail = _in_proj(hp, w_in_b, l, N_IN_PAD)
        o_ret, st_ret = _ret_prompt(proj, tab128_p, gr, B, L)
        o_ml, st_c, st_n, st_m = _mlstm_prompt(proj, tail, gb, gm, B, L)
        ckvn, krope, q, k, v = _mla_prep(tail, tab64_p, gq, gkv, wqn, wqr, (wk, wv), l, sample=False)
        o_mla = _attn_prompt(q, k, v, tail, B, L)
        res = _out_proj(o_ret, o_ml, o_mla, xp, w_out_b, l, g_next, last)
        if last:
            y_prompt = res[0]
        else:
            xp, hp = res
        for j, t in enumerate((st_ret, st_c, st_n[:, :, 0, :], st_m[:, :, 0, 0],
                               ckvn.reshape(B, L, KV_RANK), krope.reshape(B, L, D_ROPE))):
            p_out[j].append(t)

        proj_s = tail_s = _in_proj(hs, w_in_b, l, N_IN_PAD)
        o_ret, s_ret_all = _ret_step(proj_s, tab128_s, gr, state_ret, s_ret_all, l)
        o_ml, s_c_all, st_n, st_m = _mlstm_step(proj_s, tail_s, gb, gm, state_mlstm_c, state_mlstm_n,
                                                state_mlstm_m, s_c_all, l)
        ckvn, krope, qs = _mla_prep(tail_s, tab64_s, gq, gkv, wqn, wqr, (wukt,), l, sample=True)
        o_lat = _decode(page_table, jnp.transpose(qs, (1, 0, 2)), ckvn, krope, cache_ckv, cache_krope_t, l)
        o_mla = _mla_out(jnp.transpose(o_lat, (1, 0, 2)), wuv_h, l, tail_s)
        res = _out_proj(o_ret, o_ml, o_mla, xs, w_out_b, l, g_next, last)
        if last:
            y_sample = res[0]
        else:
            xs, hs = res
        for j, t in enumerate((st_n, st_m.reshape(DB, H_ML, HEAD_DIM)[:, :, 0],
                               ckvn.reshape(DB, Ls, KV_RANK), krope.reshape(DB, Ls, D_ROPE))):
            s_out[j].append(t)

    return (y_prompt.reshape(B, L, D), y_sample.reshape(DB, Ls, D),
            *[jnp.stack(t) for t in p_out], s_ret_all, s_c_all, *[jnp.stack(t) for t in s_out])
```
